```python
import jax, jax.numpy as jnp
from jax import lax
import numpy as np

D_MODEL = 1024
BATCH = 8
SEQ = 2048
DEPTH = 4

CHUNK = 64
Q_BLOCK = 128
N_MIXERS = 2
N_MLA_LAYERS = (DEPTH + 1) // 2
N_FOX_LAYERS = DEPTH // 2
D_FF = 256 * ((8 * D_MODEL // 3 + 255) // 256)
PLE_DIM = 256
MLA_HEADS = 8
MLA_NOPE = 128
MLA_ROPE = 64
MLA_V = 128
MLA_Q_LORA = 256
MLA_KV_LORA = 128
ROPE_THETA = 10000.0
FOX_HEADS = 8
FOX_HEAD_DIM = 128
NORM_EPS = 1e-6
NEG_INF = -1e30

kernel_name = "hybrid_mla_fox_macaron_trunk"


def rms_norm(x, g):
    xf = x.astype(jnp.float32)
    y = xf * lax.rsqrt(jnp.mean(xf * xf, axis=-1, keepdims=True) + NORM_EPS)
    return (y * g.astype(jnp.float32)).astype(x.dtype)


def swiglu(x, w_in, w_out):
    gate, up = jnp.split(x @ w_in, 2, axis=-1)
    return (jax.nn.silu(gate) * up) @ w_out


def apply_rope(x, cos, sin):
    x1, x2 = jnp.split(x, 2, axis=-1)
    return jnp.concatenate([x1 * cos - x2 * sin, x2 * cos + x1 * sin], axis=-1)


def causal_block_attention(scores_fn, v, chunk_causal):
    seq = v.shape[1]
    outs = []
    for q0 in range(0, seq, Q_BLOCK):
        q1 = q0 + Q_BLOCK
        s = scores_fn(q0, q1)
        q_pos = jnp.arange(q0, q1)
        k_pos = jnp.arange(q1)
        if chunk_causal:
            allowed = (k_pos[None, :] // CHUNK) <= (q_pos[:, None] // CHUNK)
        else:
            allowed = k_pos[None, :] <= q_pos[:, None]
        s = jnp.where(allowed, s, NEG_INF)
        probs = jax.nn.softmax(s, axis=-1).astype(v.dtype)
        outs.append(jnp.einsum('bhqk,bkhd->bqhd', probs, v[:, :q1]))
    return jnp.concatenate(outs, axis=1)


def mla_mixer(u, cos, sin, w_down, q_norm, w_uq, kv_norm, w_ukv, w_o):
    b, s, _ = u.shape
    c = u @ w_down
    c_q, c_kv, k_rope = jnp.split(c, [MLA_Q_LORA, MLA_Q_LORA + MLA_KV_LORA], axis=-1)
    q = (rms_norm(c_q, q_norm) @ w_uq).reshape(b, s, MLA_HEADS, MLA_NOPE + MLA_ROPE)
    q_nope, q_rope = jnp.split(q, [MLA_NOPE], axis=-1)
    q_rope = apply_rope(q_rope, cos[:, :, None, :], sin[:, :, None, :])
    k_rope = apply_rope(k_rope, cos, sin)
    kv = (rms_norm(c_kv, kv_norm) @ w_ukv).reshape(b, s, MLA_HEADS, MLA_NOPE + MLA_V)
    k_nope, v = jnp.split(kv, [MLA_NOPE], axis=-1)
    scale = (MLA_NOPE + MLA_ROPE) ** -0.5

    def scores(q0, q1):
        sc = (jnp.einsum('bqhd,bkhd->bhqk', q_nope[:, q0:q1], k_nope[:, :q1])
              + jnp.einsum('bqhr,bkr->bhqk', q_rope[:, q0:q1], k_rope[:, :q1]))
        return sc.astype(jnp.float32) * scale

    o = causal_block_attention(scores, v, chunk_causal=True)
    return o.reshape(b, s, MLA_HEADS * MLA_V) @ w_o


def fox_mixer(u, w_in, b_f, w_o):
    b, s, _ = u.shape
    hd = FOX_HEADS * FOX_HEAD_DIM
    proj = u @ w_in
    q, k, v, f_logit = jnp.split(proj, [hd, 2 * hd, 3 * hd], axis=-1)
    q = q.reshape(b, s, FOX_HEADS, FOX_HEAD_DIM)
    k = k.reshape(b, s, FOX_HEADS, FOX_HEAD_DIM)
    v = v.reshape(b, s, FOX_HEADS, FOX_HEAD_DIM)
    log_f = jax.nn.log_sigmoid(f_logit.astype(jnp.float32) + b_f.astype(jnp.float32))
    cum = jnp.cumsum(log_f, axis=1).transpose(0, 2, 1)
    scale = FOX_HEAD_DIM ** -0.5

    def scores(q0, q1):
        sc = jnp.einsum('bqhd,bkhd->bhqk', q[:, q0:q1], k[:, :q1]).astype(jnp.float32) * scale
        return sc + cum[:, :, q0:q1, None] - cum[:, :, None, :q1]

    o = causal_block_attention(scores, v, chunk_causal=False)
    return o.reshape(b, s, hd) @ w_o


def setup_inputs(seed: int = 0) -> dict:
    key = jax.random.key(seed)
    ks = jax.random.split(key, 20)
    f32 = jnp.float32

    def dense(k, shape, fan_in):
        return jax.random.normal(k, shape, f32) * (fan_in ** -0.5)

    def gain(k, shape):
        return 1.0 + 0.05 * jax.random.normal(k, shape, f32)

    x = jax.random.normal(ks[0], (BATCH, SEQ, D_MODEL), f32)
    p = jax.random.normal(ks[1], (DEPTH, BATCH, SEQ, PLE_DIM), f32)
    offset = jax.random.randint(ks[2], (BATCH, 1), 0, 4096, dtype=jnp.int32)
    positions = (offset + jnp.arange(SEQ, dtype=jnp.int32)[None, :]).astype(jnp.int32)

    ffn_norm = gain(ks[3], (DEPTH, 2, D_MODEL))
    ffn_w_in = dense(ks[4], (DEPTH, 2, D_MODEL, 2 * D_FF), D_MODEL)
    ffn_w_out = dense(ks[5], (DEPTH, 2, D_FF, D_MODEL), D_FF)
    mix_norm = gain(ks[6], (DEPTH, D_MODEL))

    mla_w_down = dense(ks[7], (N_MLA_LAYERS, D_MODEL, MLA_Q_LORA + MLA_KV_LORA + MLA_ROPE), D_MODEL)
    mla_q_norm = gain(ks[8], (N_MLA_LAYERS, MLA_Q_LORA))
    mla_w_uq = dense(ks[9], (N_MLA_LAYERS, MLA_Q_LORA, MLA_HEADS * (MLA_NOPE + MLA_ROPE)), MLA_Q_LORA)
    mla_kv_norm = gain(ks[10], (N_MLA_LAYERS, MLA_KV_LORA))
    mla_w_ukv = dense(ks[11], (N_MLA_LAYERS, MLA_KV_LORA, MLA_HEADS * (MLA_NOPE + MLA_V)), MLA_KV_LORA)
    mla_w_o = dense(ks[12], (N_MLA_LAYERS, MLA_HEADS * MLA_V, D_MODEL), MLA_HEADS * MLA_V)

    fox_hd = FOX_HEADS * FOX_HEAD_DIM
    fox_w_in = dense(ks[13], (N_FOX_LAYERS, D_MODEL, 3 * fox_hd + FOX_HEADS), D_MODEL)
    fox_w_in = fox_w_in.at[..., 3 * fox_hd:].multiply(0.1)
    fox_b_f = jax.random.uniform(ks[14], (N_FOX_LAYERS, FOX_HEADS), f32, minval=1.0, maxval=6.0)
    fox_w_o = dense(ks[15], (N_FOX_LAYERS, fox_hd, D_MODEL), fox_hd)

    ple_norm = gain(ks[16], (DEPTH, D_MODEL))
    ple_w_gate = dense(ks[17], (DEPTH, D_MODEL, D_MODEL), D_MODEL)
    ple_w_proj = dense(ks[18], (DEPTH, PLE_DIM, D_MODEL), PLE_DIM)
    final_norm = gain(ks[19], (D_MODEL,))

    return {"x": x, "p": p, "positions": positions,
            "ffn_norm": ffn_norm, "ffn_w_in": ffn_w_in, "ffn_w_out": ffn_w_out,
            "mix_norm": mix_norm,
            "mla_w_down": mla_w_down, "mla_q_norm": mla_q_norm, "mla_w_uq": mla_w_uq,
            "mla_kv_norm": mla_kv_norm, "mla_w_ukv": mla_w_ukv, "mla_w_o": mla_w_o,
            "fox_w_in": fox_w_in, "fox_b_f": fox_b_f, "fox_w_o": fox_w_o,
            "ple_norm": ple_norm, "ple_w_gate": ple_w_gate, "ple_w_proj": ple_w_proj,
            "final_norm": final_norm}


def reference(x, p, positions, ffn_norm, ffn_w_in, ffn_w_out, mix_norm,
              mla_w_down, mla_q_norm, mla_w_uq, mla_kv_norm, mla_w_ukv, mla_w_o,
              fox_w_in, fox_b_f, fox_w_o, ple_norm, ple_w_gate, ple_w_proj, final_norm):
    inv_freq = ROPE_THETA ** (-jnp.arange(0, MLA_ROPE, 2, dtype=jnp.float32) / MLA_ROPE)
    ang = positions.astype(jnp.float32)[..., None] * inv_freq
    cos = jnp.cos(ang).astype(x.dtype)
    sin = jnp.sin(ang).astype(x.dtype)

    h = x
    for i in range(DEPTH):
        h = h + 0.5 * swiglu(rms_norm(h, ffn_norm[i, 0]), ffn_w_in[i, 0], ffn_w_out[i, 0])
        u = rms_norm(h, mix_norm[i])
        j = i // N_MIXERS
        if i % N_MIXERS == 0:
            h = h + mla_mixer(u, cos, sin, mla_w_down[j], mla_q_norm[j], mla_w_uq[j],
                              mla_kv_norm[j], mla_w_ukv[j], mla_w_o[j])
        else:
            h = h + fox_mixer(u, fox_w_in[j], fox_b_f[j], fox_w_o[j])
        h = h + 0.5 * swiglu(rms_norm(h, ffn_norm[i, 1]), ffn_w_in[i, 1], ffn_w_out[i, 1])
        gate = jax.nn.sigmoid(rms_norm(h, ple_norm[i]) @ ple_w_gate[i])
        h = h + gate * (p[i] @ ple_w_proj[i])
    return rms_norm(h, final_norm)
```

```python
import functools

import jax
import jax.numpy as jnp
from jax import lax
from jax.experimental import pallas as pl
from jax.experimental.pallas import tpu as pltpu

D_MODEL = 1024
DEPTH = 4
CHUNK = 64
D_FF = 2816
PLE_DIM = 256
HEADS = 8
MLA_NOPE = 128
MLA_ROPE = 64
MLA_V = 128
MLA_Q_LORA = 256
MLA_KV_LORA = 128
ROPE_THETA = 10000.0
FOX_HEAD_DIM = 128
NORM_EPS = 1e-6
NEG_INF = -1e30

F32 = jnp.float32
BF16 = jnp.bfloat16

LANES = 128
QK_DIM = 2 * LANES
V_DIM = 128
TM = 512
FFN_COLS = 256
TQ = 256
TK = 256
VMEM_LIMIT = 56 * 1024 * 1024


def _rms(x, g):
    return x * lax.rsqrt(jnp.mean(x * x, axis=-1, keepdims=True) + NORM_EPS) * g


def _sigmoid(x):
    return 1.0 / (1.0 + jnp.exp(-x))


def _ffn_half_step(h, g, w_in_ref, w_out_ref, act_ref):
    xn = _rms(h, g).astype(BF16)
    for c in range(D_FF // FFN_COLS):
        lo = c * FFN_COLS
        gate = jnp.dot(xn, w_in_ref[:, lo:lo + FFN_COLS], preferred_element_type=F32)
        up = jnp.dot(xn, w_in_ref[:, D_FF + lo:D_FF + lo + FFN_COLS],
                     preferred_element_type=F32)
        act_ref[:, lo:lo + FFN_COLS] = (gate * _sigmoid(gate) * up).astype(BF16)
    y = jnp.dot(act_ref[...], w_out_ref[...], preferred_element_type=F32)
    return h + 0.5 * y


def _resident(block_shape, index):
    return pl.BlockSpec(block_shape, lambda *_: index, pipeline_mode=pl.Buffered(1))


def _params(n_axes):
    return pltpu.CompilerParams(
        dimension_semantics=("arbitrary",) * n_axes, vmem_limit_bytes=VMEM_LIMIT)


def _ffn_kernel(h_ref, g_ref, w_in_ref, w_out_ref, o_ref, act_ref):
    o_ref[...] = _ffn_half_step(h_ref[...], g_ref[...], w_in_ref, w_out_ref, act_ref)


def _ffn_call(h, ffn_norm, w_in, w_out, layer, half):
    t = h.shape[0]
    return pl.pallas_call(
        _ffn_kernel,
        grid=(t // TM,),
        in_specs=[
            pl.BlockSpec((TM, D_MODEL), lambda i: (i, 0)),
            _resident((None, None, 1, D_MODEL), (layer, half, 0, 0)),
            _resident((None, None, D_MODEL, 2 * D_FF), (layer, half, 0, 0)),
            _resident((None, None, D_FF, D_MODEL), (layer, half, 0, 0)),
        ],
        out_specs=pl.BlockSpec((TM, D_MODEL), lambda i: (i, 0)),
        out_shape=jax.ShapeDtypeStruct((t, D_MODEL), F32),
        scratch_shapes=[pltpu.VMEM((TM, D_FF), BF16)],
        compiler_params=_params(1),
        name="ffn",
    )(h, ffn_norm, w_in, w_out)


def _mla_proj_kernel(h_ref, pos_ref, g_ref, wd_ref, qn_ref, wuq_ref, kvn_ref, wukv_ref,
                     invf_ref, q_ref, k_ref, v_ref):
    scale = (MLA_NOPE + MLA_ROPE) ** -0.5
    u = _rms(h_ref[...], g_ref[...]).astype(BF16)
    c = jnp.dot(u, wd_ref[...], preferred_element_type=F32)
    cq = _rms(c[:, :MLA_Q_LORA], qn_ref[...]).astype(BF16)
    ckv = _rms(c[:, MLA_Q_LORA:MLA_Q_LORA + MLA_KV_LORA], kvn_ref[...]).astype(BF16)
    k_rot = c[:, MLA_Q_LORA + MLA_KV_LORA:]

    ang = pos_ref[...].astype(F32) * invf_ref[...]
    cosv = jnp.cos(ang)
    sinv = jnp.sin(ang)
    lane = lax.broadcasted_iota(jnp.int32, (TM, LANES), 1)
    half = MLA_ROPE // 2
    cos_t = jnp.where(lane < MLA_ROPE, cosv, 0.0)
    sin_up = jnp.where((lane >= half) & (lane < MLA_ROPE), sinv, 0.0)
    sin_dn = jnp.where(lane < half, -sinv, 0.0)

    def rope(x):
        return (x * cos_t + pltpu.roll(x, half, 1) * sin_up
                + pltpu.roll(x, LANES - half, 1) * sin_dn)

    k_rope = rope(k_rot).astype(BF16)
    q_lin = jnp.dot(cq, wuq_ref[...], preferred_element_type=F32)
    kv = jnp.dot(ckv, wukv_ref[...], preferred_element_type=F32)
    for hh in range(HEADS):
        q_nope = q_lin[:, hh * LANES:(hh + 1) * LANES]
        q_rot = q_lin[:, (HEADS + hh) * LANES:(HEADS + hh + 1) * LANES]
        q_ref[:, hh * QK_DIM:hh * QK_DIM + LANES] = (q_nope * scale).astype(BF16)
        q_ref[:, hh * QK_DIM + LANES:(hh + 1) * QK_DIM] = (rope(q_rot) * scale).astype(BF16)
        k_ref[:, hh * QK_DIM:hh * QK_DIM + LANES] = kv[:, hh * 256:hh * 256 + 128].astype(BF16)
        k_ref[:, hh * QK_DIM + LANES:(hh + 1) * QK_DIM] = k_rope
        v_ref[:, hh * V_DIM:(hh + 1) * V_DIM] = kv[:, hh * 256 + 128:(hh + 1) * 256].astype(BF16)


def _mla_proj_call(h, pos, mix_norm, wd, qn, wuq, kvn, wukv, invf, layer, j):
    t = h.shape[0]
    tok = lambda w: pl.BlockSpec((TM, w), lambda i: (i, 0))
    return pl.pallas_call(
        _mla_proj_kernel,
        grid=(t // TM,),
        in_specs=[
            tok(D_MODEL),
            tok(1),
            _resident((None, 1, D_MODEL), (layer, 0, 0)),
            _resident((None, D_MODEL, 512), (j, 0, 0)),
            _resident((None, 1, MLA_Q_LORA), (j, 0, 0)),
            _resident((None, MLA_Q_LORA, 2 * HEADS * LANES), (j, 0, 0)),
            _resident((None, 1, MLA_KV_LORA), (j, 0, 0)),
            _resident((None, MLA_KV_LORA, HEADS * (MLA_NOPE + MLA_V)), (j, 0, 0)),
            _resident((1, LANES), (0, 0)),
        ],
        out_specs=[tok(HEADS * QK_DIM), tok(HEADS * QK_DIM), tok(HEADS * V_DIM)],
        out_shape=[
            jax.ShapeDtypeStruct((t, HEADS * QK_DIM), BF16),
            jax.ShapeDtypeStruct((t, HEADS * QK_DIM), BF16),
            jax.ShapeDtypeStruct((t, HEADS * V_DIM), BF16),
        ],
        compiler_params=_params(1),
        name="mla_proj",
    )(h, pos, mix_norm, wd, qn, wuq, kvn, wukv, invf)


def _split3(x):
    hi = x.astype(BF16).astype(F32)
    r = x - hi
    mid = r.astype(BF16).astype(F32)
    lo = r - mid
    return hi, mid, lo


def _fox_proj_kernel(h_ref, g_ref, w_ref, bf_ref, q_ref, k_ref, v_ref, carry_ref):
    hd = HEADS * FOX_HEAD_DIM
    scale = FOX_HEAD_DIM ** -0.5

    @pl.when(pl.program_id(1) == 0)
    def _():
        carry_ref[...] = jnp.zeros_like(carry_ref)

    u = _rms(h_ref[...], g_ref[...]).astype(BF16)
    q = jnp.dot(u, w_ref[:, :hd], preferred_element_type=F32)
    k = jnp.dot(u, w_ref[:, hd:2 * hd], preferred_element_type=F32)
    v = jnp.dot(u, w_ref[:, 2 * hd:3 * hd], preferred_element_type=F32)
    f_logit = jnp.dot(u, w_ref[:, 3 * hd:], preferred_element_type=F32)
    z = f_logit + bf_ref[...]
    log_f = -(jnp.maximum(-z, 0.0) + jnp.log1p(jnp.exp(-jnp.abs(z))))

    r = lax.broadcasted_iota(jnp.int32, (TM, TM), 0)
    cc = lax.broadcasted_iota(jnp.int32, (TM, TM), 1)
    tri = (cc <= r).astype(BF16)
    hi, mid, lo = _split3(log_f)
    psum = lambda part: jnp.dot(tri, part.astype(BF16), preferred_element_type=F32)
    cum = (psum(lo) + psum(mid)) + psum(hi) + carry_ref[...]
    carry_ref[...] = cum[TM - 1:TM, :]

    lane = lax.broadcasted_iota(jnp.int32, (TM, LANES), 1)
    for hh in range(HEADS):
        cb = jnp.broadcast_to(cum[:, hh:hh + 1], (TM, LANES))
        c_hi, c_mid, c_lo = _split3(cb)
        parts = jnp.where((lane == 0) | (lane == 3), c_hi,
                          jnp.where((lane == 1) | (lane == 4), c_mid, c_lo))
        q_ex = jnp.where(lane < 3, parts, jnp.where(lane < 6, 1.0, 0.0))
        k_ex = jnp.where(lane < 3, 1.0, jnp.where(lane < 6, -parts, 0.0))
        q_ref[:, hh * QK_DIM:hh * QK_DIM + LANES] = (
            q[:, hh * LANES:(hh + 1) * LANES] * scale).astype(BF16)
        q_ref[:, hh * QK_DIM + LANES:(hh + 1) * QK_DIM] = q_ex.astype(BF16)
        k_ref[:, hh * QK_DIM:hh * QK_DIM + LANES] = k[:, hh * LANES:(hh + 1) * LANES].astype(BF16)
        k_ref[:, hh * QK_DIM + LANES:(hh + 1) * QK_DIM] = k_ex.astype(BF16)
    v_ref[...] = v.astype(BF16)


def _fox_proj_call(h, mix_norm, w_in, b_f, layer, j, batch, seq):
    t = h.shape[0]
    nt = seq // TM
    tok = lambda w: pl.BlockSpec((TM, w), lambda b, i: (b * nt + i, 0))
    return pl.pallas_call(
        _fox_proj_kernel,
        grid=(batch, nt),
        in_specs=[
            tok(D_MODEL),
            _resident((None, 1, D_MODEL), (layer, 0, 0)),
            _resident((None, D_MODEL, 3 * HEADS * FOX_HEAD_DIM + LANES), (j, 0, 0)),
            _resident((None, 1, LANES), (j, 0, 0)),
        ],
        out_specs=[tok(HEADS * QK_DIM), tok(HEADS * QK_DIM), tok(HEADS * V_DIM)],
        out_shape=[
            jax.ShapeDtypeStruct((t, HEADS * QK_DIM), BF16),
            jax.ShapeDtypeStruct((t, HEADS * QK_DIM), BF16),
            jax.ShapeDtypeStruct((t, HEADS * V_DIM), BF16),
        ],
        scratch_shapes=[pltpu.VMEM((1, LANES), F32)],
        compiler_params=_params(2),
        name="fox_proj",
    )(h, mix_norm, w_in, b_f)


def _attn_kernel(q_ref, k_ref, v_ref, o_ref, *, chunk_causal, seq):
    row = lax.broadcasted_iota(jnp.int32, (TQ, TK), 0)
    col = lax.broadcasted_iota(jnp.int32, (TQ, TK), 1)
    if chunk_causal:
        shift = CHUNK.bit_length() - 1
        allowed = (col >> shift) <= (row >> shift)
    else:
        allowed = col <= row

    def tile(q, k0, carry, diagonal):
        m, l, acc = carry
        k = k_ref[pl.ds(k0, TK), :]
        v = v_ref[pl.ds(k0, TK), :]
        s = lax.dot_general(q, k, (((1,), (1,)), ((), ())), preferred_element_type=F32)
        if diagonal:
            s = jnp.where(allowed, s, NEG_INF)
        m_new = jnp.maximum(m, jnp.max(s, axis=-1, keepdims=True))
        alpha = jnp.exp(m - m_new)
        p = jnp.exp(s - m_new)
        l = alpha * l + jnp.sum(p, axis=-1, keepdims=True)
        acc = alpha * acc + jnp.dot(p.astype(BF16), v, preferred_element_type=F32)
        return m_new, l, acc

    def q_block(i, _):
        r0 = pl.multiple_of(i * TQ, TQ)
        q = q_ref[pl.ds(r0, TQ), :]
        init = (jnp.full((TQ, 1), NEG_INF, F32), jnp.zeros((TQ, 1), F32),
                jnp.zeros((TQ, V_DIM), F32))
        carry = lax.fori_loop(
            0, i, lambda j, cr: tile(q, pl.multiple_of(j * TK, TK), cr, False), init)
        _, l, acc = tile(q, r0, carry, True)
        o_ref[pl.ds(r0, TQ), :] = (acc / l).astype(o_ref.dtype)
        return 0

    lax.fori_loop(0, seq // TQ, q_block, 0)


def _attn_call(q, k, v, chunk_causal):
    batch, seq, _ = q.shape
    return pl.pallas_call(
        functools.partial(_attn_kernel, chunk_causal=chunk_causal, seq=seq),
        grid=(batch, HEADS),
        in_specs=[
            pl.BlockSpec((None, seq, QK_DIM), lambda b, h: (b, 0, h)),
            pl.BlockSpec((None, seq, QK_DIM), lambda b, h: (b, 0, h)),
            pl.BlockSpec((None, seq, V_DIM), lambda b, h: (b, 0, h)),
        ],
        out_specs=pl.BlockSpec((None, seq, V_DIM), lambda b, h: (b, 0, h)),
        out_shape=jax.ShapeDtypeStruct((batch, seq, HEADS * V_DIM), BF16),
        compiler_params=_params(2),
        name="attn",
    )(q, k, v)


def _post_kernel(h_ref, a_ref, p_ref, wo_ref, g_ref, w_in_ref, w_out_ref, pg_ref, wg_ref,
                 wp_ref, fn_ref, o_ref, act_ref, *, final):
    h = h_ref[...] + jnp.dot(a_ref[...], wo_ref[...], preferred_element_type=F32)
    h = _ffn_half_step(h, g_ref[...], w_in_ref, w_out_ref, act_ref)
    gate = _sigmoid(jnp.dot(_rms(h, pg_ref[...]).astype(BF16), wg_ref[...],
                            preferred_element_type=F32))
    emb = jnp.dot(p_ref[...].astype(BF16), wp_ref[...], preferred_element_type=F32)
    h = h + gate * emb
    if final:
        h = _rms(h, fn_ref[...])
    o_ref[...] = h


def _post_call(h, attn, p, w_o, ffn_norm, w_in, w_out, ple_norm, w_gate, w_proj, final_norm,
               layer, j, final):
    t = h.shape[0]
    return pl.pallas_call(
        functools.partial(_post_kernel, final=final),
        grid=(t // TM,),
        in_specs=[
            pl.BlockSpec((TM, D_MODEL), lambda i: (i, 0)),
            pl.BlockSpec((TM, HEADS * V_DIM), lambda i: (i, 0)),
            pl.BlockSpec((None, TM, PLE_DIM), lambda i: (layer, i, 0)),
            _resident((None, HEADS * V_DIM, D_MODEL), (j, 0, 0)),
            _resident((None, None, 1, D_MODEL), (layer, 1, 0, 0)),
            _resident((None, None, D_MODEL, 2 * D_FF), (layer, 1, 0, 0)),
            _resident((None, None, D_FF, D_MODEL), (layer, 1, 0, 0)),
            _resident((None, 1, D_MODEL), (layer, 0, 0)),
            _resident((None, D_MODEL, D_MODEL), (layer, 0, 0)),
            _resident((None, PLE_DIM, D_MODEL), (layer, 0, 0)),
            _resident((1, D_MODEL), (0, 0)),
        ],
        out_specs=pl.BlockSpec((TM, D_MODEL), lambda i: (i, 0)),
        out_shape=jax.ShapeDtypeStruct((t, D_MODEL), F32),
        scratch_shapes=[pltpu.VMEM((TM, D_FF), BF16)],
        compiler_params=_params(1),
        name="post",
    )(h, attn, p, w_o, ffn_norm, w_in, w_out, ple_norm, w_gate, w_proj, final_norm)


def kernel(x, p, positions, ffn_norm, ffn_w_in, ffn_w_out, mix_norm, mla_w_down, mla_q_norm,
           mla_w_uq, mla_kv_norm, mla_w_ukv, mla_w_o, fox_w_in, fox_b_f, fox_w_o, ple_norm,
           ple_w_gate, ple_w_proj, final_norm):
    batch, seq, _ = x.shape
    t = batch * seq
    n_mla = mla_w_down.shape[0]
    n_fox = fox_w_in.shape[0]

    w_in = ffn_w_in.astype(BF16)
    w_out = ffn_w_out.astype(BF16)
    ffn_g = ffn_norm.reshape(DEPTH, 2, 1, D_MODEL)
    mix_g = mix_norm.reshape(DEPTH, 1, D_MODEL)
    ple_g = ple_norm.reshape(DEPTH, 1, D_MODEL)
    fin_g = final_norm.reshape(1, D_MODEL)
    w_gate = ple_w_gate.astype(BF16)
    w_proj = ple_w_proj.astype(BF16)

    wd = jnp.pad(mla_w_down, ((0, 0), (0, 0), (0, LANES - MLA_ROPE))).astype(BF16)
    wq = mla_w_uq.reshape(n_mla, MLA_Q_LORA, HEADS, MLA_NOPE + MLA_ROPE)
    wq_nope = wq[..., :MLA_NOPE].reshape(n_mla, MLA_Q_LORA, HEADS * LANES)
    wq_rope = jnp.pad(wq[..., MLA_NOPE:], ((0, 0), (0, 0), (0, 0), (0, LANES - MLA_ROPE)))
    wuq = jnp.concatenate(
        [wq_nope, wq_rope.reshape(n_mla, MLA_Q_LORA, HEADS * LANES)], axis=-1).astype(BF16)
    wukv = mla_w_ukv.astype(BF16)
    q_g = mla_q_norm.reshape(n_mla, 1, MLA_Q_LORA)
    kv_g = mla_kv_norm.reshape(n_mla, 1, MLA_KV_LORA)
    mla_wo = mla_w_o.astype(BF16)
    inv_freq = ROPE_THETA ** (-jnp.arange(0, MLA_ROPE, 2, dtype=F32) / MLA_ROPE)
    invf = jnp.concatenate([inv_freq, inv_freq, jnp.zeros((LANES - MLA_ROPE,), F32)])[None, :]

    fox_w = jnp.pad(fox_w_in, ((0, 0), (0, 0), (0, LANES - HEADS))).astype(BF16)
    fox_b = jnp.pad(fox_b_f, ((0, 0), (0, LANES - HEADS))).reshape(n_fox, 1, LANES)
    fox_wo = fox_w_o.astype(BF16)

    pos = positions.reshape(t, 1)
    p_tok = p.reshape(DEPTH, t, PLE_DIM)

    h = x.reshape(t, D_MODEL)
    for i in range(DEPTH):
        j = i // 2
        h = _ffn_call(h, ffn_g, w_in, w_out, i, 0)
        if i % 2 == 0:
            q, k, v = _mla_proj_call(h, pos, mix_g, wd, q_g, wuq, kv_g, wukv, invf, i, j)
            w_o = mla_wo
        else:
            q, k, v = _fox_proj_call(h, mix_g, fox_w, fox_b, i, j, batch, seq)
            w_o = fox_wo
        attn = _attn_call(q.reshape(batch, seq, -1), k.reshape(batch, seq, -1),
                          v.reshape(batch, seq, -1), chunk_causal=(i % 2 == 0))
        h = _post_call(h, attn.reshape(t, -1), p_tok, w_o, ffn_g, w_in, w_out, ple_g, w_gate,
                       w_proj, fin_g, i, j, final=(i == DEPTH - 1))
    return h.reshape(batch, seq, D_MODEL)
```

```python
import functools

import jax
import jax.numpy as jnp
from jax import lax
from jax.experimental import pallas as pl
from jax.experimental.pallas import tpu as pltpu

D_MODEL = 1024
DEPTH = 4
CHUNK = 64
D_FF = 2816
PLE_DIM = 256
HEADS = 8
MLA_NOPE = 128
MLA_ROPE = 64
MLA_V = 128
MLA_Q_LORA = 256
MLA_KV_LORA = 128
ROPE_THETA = 10000.0
FOX_HEAD_DIM = 128
NORM_EPS = 1e-6
NEG_INF = -1e30

F32 = jnp.float32
BF16 = jnp.bfloat16

LANES = 128
QK_DIM = 2 * LANES
V_DIM = 128
TM = 512
FFN_COLS = 256
TQ = 256
VMEM_LIMIT = 56 * 1024 * 1024


def _rms(x, g):
    return x * lax.rsqrt(jnp.mean(x * x, axis=-1, keepdims=True) + NORM_EPS) * g


def _sigmoid(x):
    return 1.0 / (1.0 + jnp.exp(-x))


def _ffn_half_step(h, g, w_in_ref, w_out_ref, act_ref):
    xn = _rms(h, g).astype(BF16)
    for c in range(D_FF // FFN_COLS):
        lo = c * FFN_COLS
        gate = jnp.dot(xn, w_in_ref[:, lo:lo + FFN_COLS], preferred_element_type=F32)
        up = jnp.dot(xn, w_in_ref[:, D_FF + lo:D_FF + lo + FFN_COLS],
                     preferred_element_type=F32)
        act_ref[:, lo:lo + FFN_COLS] = (gate * _sigmoid(gate) * up).astype(BF16)
    y = jnp.dot(act_ref[...], w_out_ref[...], preferred_element_type=F32)
    return h + 0.5 * y


def _resident(block_shape, index):
    return pl.BlockSpec(block_shape, lambda *_: index, pipeline_mode=pl.Buffered(1))


def _params(n_axes):
    return pltpu.CompilerParams(
        dimension_semantics=("arbitrary",) * n_axes, vmem_limit_bytes=VMEM_LIMIT)


def _ffn_kernel(h_ref, g_ref, w_in_ref, w_out_ref, o_ref, act_ref):
    o_ref[...] = _ffn_half_step(h_ref[...], g_ref[...], w_in_ref, w_out_ref, act_ref)


def _ffn_call(h, ffn_norm, w_in, w_out, layer, half):
    t = h.shape[0]
    return pl.pallas_call(
        _ffn_kernel,
        grid=(t // TM,),
        in_specs=[
            pl.BlockSpec((TM, D_MODEL), lambda i: (i, 0)),
            _resident((None, None, 1, D_MODEL), (layer, half, 0, 0)),
            _resident((None, None, D_MODEL, 2 * D_FF), (layer, half, 0, 0)),
            _resident((None, None, D_FF, D_MODEL), (layer, half, 0, 0)),
        ],
        out_specs=pl.BlockSpec((TM, D_MODEL), lambda i: (i, 0)),
        out_shape=jax.ShapeDtypeStruct((t, D_MODEL), F32),
        scratch_shapes=[pltpu.VMEM((TM, D_FF), BF16)],
        compiler_params=_params(1),
        name="ffn",
    )(h, ffn_norm, w_in, w_out)


def _mla_proj_kernel(h_ref, pos_ref, g_ref, wd_ref, qn_ref, wuq_ref, kvn_ref, wukv_ref,
                     invf_ref, q_ref, k_ref, v_ref):
    scale = (MLA_NOPE + MLA_ROPE) ** -0.5
    u = _rms(h_ref[...], g_ref[...]).astype(BF16)
    c = jnp.dot(u, wd_ref[...], preferred_element_type=F32)
    cq = _rms(c[:, :MLA_Q_LORA], qn_ref[...]).astype(BF16)
    ckv = _rms(c[:, MLA_Q_LORA:MLA_Q_LORA + MLA_KV_LORA], kvn_ref[...]).astype(BF16)
    k_rot = c[:, MLA_Q_LORA + MLA_KV_LORA:]

    ang = pos_ref[...].astype(F32) * invf_ref[...]
    cosv = jnp.cos(ang)
    sinv = jnp.sin(ang)
    lane = lax.broadcasted_iota(jnp.int32, (TM, LANES), 1)
    half = MLA_ROPE // 2
    cos_t = jnp.where(lane < MLA_ROPE, cosv, 0.0)
    sin_up = jnp.where((lane >= half) & (lane < MLA_ROPE), sinv, 0.0)
    sin_dn = jnp.where(lane < half, -sinv, 0.0)

    def rope(x):
        return (x * cos_t + pltpu.roll(x, half, 1) * sin_up
                + pltpu.roll(x, LANES - half, 1) * sin_dn)

    k_rope = rope(k_rot).astype(BF16)
    q_lin = jnp.dot(cq, wuq_ref[...], preferred_element_type=F32)
    kv = jnp.dot(ckv, wukv_ref[...], preferred_element_type=F32)
    for hh in range(HEADS):
        q_nope = q_lin[:, hh * LANES:(hh + 1) * LANES]
        q_rot = q_lin[:, (HEADS + hh) * LANES:(HEADS + hh + 1) * LANES]
        q_ref[:, hh * QK_DIM:hh * QK_DIM + LANES] = (q_nope * scale).astype(BF16)
        q_ref[:, hh * QK_DIM + LANES:(hh + 1) * QK_DIM] = (rope(q_rot) * scale).astype(BF16)
        k_ref[:, hh * QK_DIM:hh * QK_DIM + LANES] = kv[:, hh * 256:hh * 256 + 128].astype(BF16)
        k_ref[:, hh * QK_DIM + LANES:(hh + 1) * QK_DIM] = k_rope
        v_ref[:, hh * V_DIM:(hh + 1) * V_DIM] = kv[:, hh * 256 + 128:(hh + 1) * 256].astype(BF16)


def _mla_proj_call(h, pos, mix_norm, wd, qn, wuq, kvn, wukv, invf, layer, j):
    t = h.shape[0]
    tok = lambda w: pl.BlockSpec((TM, w), lambda i: (i, 0))
    return pl.pallas_call(
        _mla_proj_kernel,
        grid=(t // TM,),
        in_specs=[
            tok(D_MODEL),
            tok(1),
            _resident((None, 1, D_MODEL), (layer, 0, 0)),
            _resident((None, D_MODEL, 512), (j, 0, 0)),
            _resident((None, 1, MLA_Q_LORA), (j, 0, 0)),
            _resident((None, MLA_Q_LORA, 2 * HEADS * LANES), (j, 0, 0)),
            _resident((None, 1, MLA_KV_LORA), (j, 0, 0)),
            _resident((None, MLA_KV_LORA, HEADS * (MLA_NOPE + MLA_V)), (j, 0, 0)),
            _resident((1, LANES), (0, 0)),
        ],
        out_specs=[tok(HEADS * QK_DIM), tok(HEADS * QK_DIM), tok(HEADS * V_DIM)],
        out_shape=[
            jax.ShapeDtypeStruct((t, HEADS * QK_DIM), BF16),
            jax.ShapeDtypeStruct((t, HEADS * QK_DIM), BF16),
            jax.ShapeDtypeStruct((t, HEADS * V_DIM), BF16),
        ],
        compiler_params=_params(1),
        name="mla_proj",
    )(h, pos, mix_norm, wd, qn, wuq, kvn, wukv, invf)


def _split3(x):
    hi = x.astype(BF16).astype(F32)
    r = x - hi
    mid = r.astype(BF16).astype(F32)
    lo = r - mid
    return hi, mid, lo


def _fox_proj_kernel(h_ref, g_ref, w_ref, bf_ref, q_ref, k_ref, v_ref, carry_ref):
    hd = HEADS * FOX_HEAD_DIM
    scale = FOX_HEAD_DIM ** -0.5

    @pl.when(pl.program_id(1) == 0)
    def _():
        carry_ref[...] = jnp.zeros_like(carry_ref)

    u = _rms(h_ref[...], g_ref[...]).astype(BF16)
    q = jnp.dot(u, w_ref[:, :hd], preferred_element_type=F32)
    k = jnp.dot(u, w_ref[:, hd:2 * hd], preferred_element_type=F32)
    v = jnp.dot(u, w_ref[:, 2 * hd:3 * hd], preferred_element_type=F32)
    f_logit = jnp.dot(u, w_ref[:, 3 * hd:], preferred_element_type=F32)
    z = f_logit + bf_ref[...]
    log_f = -(jnp.maximum(-z, 0.0) + jnp.log1p(jnp.exp(-jnp.abs(z))))

    r = lax.broadcasted_iota(jnp.int32, (TM, TM), 0)
    cc = lax.broadcasted_iota(jnp.int32, (TM, TM), 1)
    tri = (cc <= r).astype(BF16)
    hi, mid, lo = _split3(log_f)
    psum = lambda part: jnp.dot(tri, part.astype(BF16), preferred_element_type=F32)
    cum = (psum(lo) + psum(mid)) + psum(hi) + carry_ref[...]
    carry_ref[...] = cum[TM - 1:TM, :]

    lane = lax.broadcasted_iota(jnp.int32, (TM, LANES), 1)
    for hh in range(HEADS):
        cb = jnp.broadcast_to(cum[:, hh:hh + 1], (TM, LANES))
        c_hi, c_mid, c_lo = _split3(cb)
        parts = jnp.where((lane == 0) | (lane == 3), c_hi,
                          jnp.where((lane == 1) | (lane == 4), c_mid, c_lo))
        q_ex = jnp.where(lane < 3, parts, jnp.where(lane < 6, 1.0, 0.0))
        k_ex = jnp.where(lane < 3, 1.0, jnp.where(lane < 6, -parts, 0.0))
        q_ref[:, hh * QK_DIM:hh * QK_DIM + LANES] = (
            q[:, hh * LANES:(hh + 1) * LANES] * scale).astype(BF16)
        q_ref[:, hh * QK_DIM + LANES:(hh + 1) * QK_DIM] = q_ex.astype(BF16)
        k_ref[:, hh * QK_DIM:hh * QK_DIM + LANES] = k[:, hh * LANES:(hh + 1) * LANES].astype(BF16)
        k_ref[:, hh * QK_DIM + LANES:(hh + 1) * QK_DIM] = k_ex.astype(BF16)
    v_ref[...] = v.astype(BF16)


def _fox_proj_call(h, mix_norm, w_in, b_f, layer, j, batch, seq):
    t = h.shape[0]
    nt = seq // TM
    tok = lambda w: pl.BlockSpec((TM, w), lambda b, i: (b * nt + i, 0))
    return pl.pallas_call(
        _fox_proj_kernel,
        grid=(batch, nt),
        in_specs=[
            tok(D_MODEL),
            _resident((None, 1, D_MODEL), (layer, 0, 0)),
            _resident((None, D_MODEL, 3 * HEADS * FOX_HEAD_DIM + LANES), (j, 0, 0)),
            _resident((None, 1, LANES), (j, 0, 0)),
        ],
        out_specs=[tok(HEADS * QK_DIM), tok(HEADS * QK_DIM), tok(HEADS * V_DIM)],
        out_shape=[
            jax.ShapeDtypeStruct((t, HEADS * QK_DIM), BF16),
            jax.ShapeDtypeStruct((t, HEADS * QK_DIM), BF16),
            jax.ShapeDtypeStruct((t, HEADS * V_DIM), BF16),
        ],
        scratch_shapes=[pltpu.VMEM((1, LANES), F32)],
        compiler_params=_params(2),
        name="fox_proj",
    )(h, mix_norm, w_in, b_f)


def _attn_kernel(q_ref, k_ref, v_ref, o_ref, vt_ref, *, chunk_causal, seq):
    key = lax.broadcasted_iota(jnp.int32, (TQ, TQ), 0)
    qry = lax.broadcasted_iota(jnp.int32, (TQ, TQ), 1)
    if chunk_causal:
        shift = CHUNK.bit_length() - 1
        allowed = (key >> shift) <= (qry >> shift)
    else:
        allowed = key <= qry
    nt = (((1,), (1,)), ((), ()))

    vt_ref[...] = v_ref[...].astype(F32).T.astype(BF16)
    for i in range(seq // TQ):
        q0 = i * TQ
        q = q_ref[q0:q0 + TQ, :]
        s_dg = lax.dot_general(k_ref[q0:q0 + TQ, :], q, nt, preferred_element_type=F32)
        s_dg = jnp.where(allowed, s_dg, NEG_INF)
        m = jnp.max(s_dg, axis=0, keepdims=True)
        if i > 0:
            s_lo = lax.dot_general(k_ref[0:q0, :], q, nt, preferred_element_type=F32)
            m = jnp.maximum(m, jnp.max(s_lo, axis=0, keepdims=True))
        p_dg = jnp.exp(s_dg - m)
        l = jnp.sum(p_dg, axis=0, keepdims=True)
        o_t = jnp.dot(vt_ref[:, q0:q0 + TQ], p_dg.astype(BF16), preferred_element_type=F32)
        if i > 0:
            p_lo = jnp.exp(s_lo - m)
            l = l + jnp.sum(p_lo, axis=0, keepdims=True)
            o_t = o_t + jnp.dot(vt_ref[:, 0:q0], p_lo.astype(BF16),
                                preferred_element_type=F32)
        o_ref[q0:q0 + TQ, :] = (o_t / l).T.astype(o_ref.dtype)


def _attn_call(q, k, v, chunk_causal):
    batch, seq, _ = q.shape
    return pl.pallas_call(
        functools.partial(_attn_kernel, chunk_causal=chunk_causal, seq=seq),
        grid=(batch, HEADS),
        in_specs=[
            pl.BlockSpec((None, seq, QK_DIM), lambda b, h: (b, 0, h)),
            pl.BlockSpec((None, seq, QK_DIM), lambda b, h: (b, 0, h)),
            pl.BlockSpec((None, seq, V_DIM), lambda b, h: (b, 0, h)),
        ],
        out_specs=pl.BlockSpec((None, seq, V_DIM), lambda b, h: (b, 0, h)),
        out_shape=jax.ShapeDtypeStruct((batch, seq, HEADS * V_DIM), BF16),
        scratch_shapes=[pltpu.VMEM((V_DIM, seq), BF16)],
        compiler_params=_params(2),
        name="attn",
    )(q, k, v)


def _post_kernel(h_ref, a_ref, p_ref, wo_ref, g_ref, w_in_ref, w_out_ref, pg_ref, wg_ref,
                 wp_ref, fn_ref, o_ref, act_ref, *, final):
    h = h_ref[...] + jnp.dot(a_ref[...], wo_ref[...], preferred_element_type=F32)
    h = _ffn_half_step(h, g_ref[...], w_in_ref, w_out_ref, act_ref)
    gate = _sigmoid(jnp.dot(_rms(h, pg_ref[...]).astype(BF16), wg_ref[...],
                            preferred_element_type=F32))
    emb = jnp.dot(p_ref[...].astype(BF16), wp_ref[...], preferred_element_type=F32)
    h = h + gate * emb
    if final:
        h = _rms(h, fn_ref[...])
    o_ref[...] = h


def _post_call(h, attn, p, w_o, ffn_norm, w_in, w_out, ple_norm, w_gate, w_proj, final_norm,
               layer, j, final):
    t = h.shape[0]
    return pl.pallas_call(
        functools.partial(_post_kernel, final=final),
        grid=(t // TM,),
        in_specs=[
            pl.BlockSpec((TM, D_MODEL), lambda i: (i, 0)),
            pl.BlockSpec((TM, HEADS * V_DIM), lambda i: (i, 0)),
            pl.BlockSpec((None, TM, PLE_DIM), lambda i: (layer, i, 0)),
            _resident((None, HEADS * V_DIM, D_MODEL), (j, 0, 0)),
            _resident((None, None, 1, D_MODEL), (layer, 1, 0, 0)),
            _resident((None, None, D_MODEL, 2 * D_FF), (layer, 1, 0, 0)),
            _resident((None, None, D_FF, D_MODEL), (layer, 1, 0, 0)),
            _resident((None, 1, D_MODEL), (layer, 0, 0)),
            _resident((None, D_MODEL, D_MODEL), (layer, 0, 0)),
            _resident((None, PLE_DIM, D_MODEL), (layer, 0, 0)),
            _resident((1, D_MODEL), (0, 0)),
        ],
        out_specs=pl.BlockSpec((TM, D_MODEL), lambda i: (i, 0)),
        out_shape=jax.ShapeDtypeStruct((t, D_MODEL), F32),
        scratch_shapes=[pltpu.VMEM((TM, D_FF), BF16)],
        compiler_params=_params(1),
        name="post",
    )(h, attn, p, w_o, ffn_norm, w_in, w_out, ple_norm, w_gate, w_proj, final_norm)


def kernel(x, p, positions, ffn_norm, ffn_w_in, ffn_w_out, mix_norm, mla_w_down, mla_q_norm,
           mla_w_uq, mla_kv_norm, mla_w_ukv, mla_w_o, fox_w_in, fox_b_f, fox_w_o, ple_norm,
           ple_w_gate, ple_w_proj, final_norm):
    batch, seq, _ = x.shape
    t = batch * seq
    n_mla = mla_w_down.shape[0]
    n_fox = fox_w_in.shape[0]

    w_in = ffn_w_in.astype(BF16)
    w_out = ffn_w_out.astype(BF16)
    ffn_g = ffn_norm.reshape(DEPTH, 2, 1, D_MODEL)
    mix_g = mix_norm.reshape(DEPTH, 1, D_MODEL)
    ple_g = ple_norm.reshape(DEPTH, 1, D_MODEL)
    fin_g = final_norm.reshape(1, D_MODEL)
    w_gate = ple_w_gate.astype(BF16)
    w_proj = ple_w_proj.astype(BF16)

    wd = jnp.pad(mla_w_down, ((0, 0), (0, 0), (0, LANES - MLA_ROPE))).astype(BF16)
    wq = mla_w_uq.reshape(n_mla, MLA_Q_LORA, HEADS, MLA_NOPE + MLA_ROPE)
    wq_nope = wq[..., :MLA_NOPE].reshape(n_mla, MLA_Q_LORA, HEADS * LANES)
    wq_rope = jnp.pad(wq[..., MLA_NOPE:], ((0, 0), (0, 0), (0, 0), (0, LANES - MLA_ROPE)))
    wuq = jnp.concatenate(
        [wq_nope, wq_rope.reshape(n_mla, MLA_Q_LORA, HEADS * LANES)], axis=-1).astype(BF16)
    wukv = mla_w_ukv.astype(BF16)
    q_g = mla_q_norm.reshape(n_mla, 1, MLA_Q_LORA)
    kv_g = mla_kv_norm.reshape(n_mla, 1, MLA_KV_LORA)
    mla_wo = mla_w_o.astype(BF16)
    inv_freq = ROPE_THETA ** (-jnp.arange(0, MLA_ROPE, 2, dtype=F32) / MLA_ROPE)
    invf = jnp.concatenate([inv_freq, inv_freq, jnp.zeros((LANES - MLA_ROPE,), F32)])[None, :]

    fox_w = jnp.pad(fox_w_in, ((0, 0), (0, 0), (0, LANES - HEADS))).astype(BF16)
    fox_b = jnp.pad(fox_b_f, ((0, 0), (0, LANES - HEADS))).reshape(n_fox, 1, LANES)
    fox_wo = fox_w_o.astype(BF16)

    pos = positions.reshape(t, 1)
    p_tok = p.reshape(DEPTH, t, PLE_DIM)

    h = x.reshape(t, D_MODEL)
    for i in range(DEPTH):
        j = i // 2
        h = _ffn_call(h, ffn_g, w_in, w_out, i, 0)
        if i % 2 == 0:
            q, k, v = _mla_proj_call(h, pos, mix_g, wd, q_g, wuq, kv_g, wukv, invf, i, j)
            w_o = mla_wo
        else:
            q, k, v = _fox_proj_call(h, mix_g, fox_w, fox_b, i, j, batch, seq)
            w_o = fox_wo
        attn = _attn_call(q.reshape(batch, seq, -1), k.reshape(batch, seq, -1),
                          v.reshape(batch, seq, -1), chunk_causal=(i % 2 == 0))
        h = _post_call(h, attn.reshape(t, -1), p_tok, w_o, ffn_g, w_in, w_out, ple_g, w_gate,
                       w_proj, fin_g, i, j, final=(i == DEPTH - 1))
    return h.reshape(batch, seq, D_MODEL)
```

```python
import functools

import jax
import jax.numpy as jnp
from jax import lax
from jax.experimental import pallas as pl
from jax.experimental.pallas import tpu as pltpu

D_MODEL = 1024
DEPTH = 4
CHUNK = 64
D_FF = 2816
PLE_DIM = 256
HEADS = 8
MLA_NOPE = 128
MLA_ROPE = 64
MLA_V = 128
MLA_Q_LORA = 256
MLA_KV_LORA = 128
ROPE_THETA = 10000.0
FOX_HEAD_DIM = 128
NORM_EPS = 1e-6
NEG_INF = -1e30

F32 = jnp.float32
BF16 = jnp.bfloat16

LANES = 128
QK_DIM = 2 * LANES
V_DIM = 128
TM = 512
FFN_COLS = 256
TQ = 256
HEADS_PER_STEP = 2
SCORE_LOOKAHEAD = 3
BF16_SUBLANES = 16
VT_ROWS = V_DIM + BF16_SUBLANES
LOG2E = 1.4426950408889634
ROPE_PACK = LANES // (MLA_ROPE // 2)
VMEM_LIMIT = 56 * 1024 * 1024


def _rms(x, g):
    return x * lax.rsqrt(jnp.mean(x * x, axis=-1, keepdims=True) + NORM_EPS) * g


def _sigmoid(x):
    return 1.0 / (1.0 + jnp.exp(-x))


def _lane_roll(x, shift):
    return x if shift % LANES == 0 else pltpu.roll(x, shift % LANES, 1)


def _ffn_half_step(h, g, w_in_ref, w_out_ref, act_ref):
    xn = _rms(h, g).astype(BF16)
    for c in range(D_FF // FFN_COLS):
        lo = c * FFN_COLS
        gate = jnp.dot(xn, w_in_ref[:, lo:lo + FFN_COLS], preferred_element_type=F32)
        up = jnp.dot(xn, w_in_ref[:, D_FF + lo:D_FF + lo + FFN_COLS],
                     preferred_element_type=F32)
        act_ref[:, lo:lo + FFN_COLS] = (gate * _sigmoid(gate) * up).astype(BF16)
    y = jnp.dot(act_ref[...], w_out_ref[...], preferred_element_type=F32)
    return h + 0.5 * y


def _resident(block_shape, index):
    return pl.BlockSpec(block_shape, lambda *_: index, pipeline_mode=pl.Buffered(1))


def _params(n_axes):
    return pltpu.CompilerParams(
        dimension_semantics=("arbitrary",) * n_axes, vmem_limit_bytes=VMEM_LIMIT)


def _ffn_kernel(h_ref, g_ref, w_in_ref, w_out_ref, o_ref, act_ref):
    o_ref[...] = _ffn_half_step(h_ref[...], g_ref[...], w_in_ref, w_out_ref, act_ref)


def _ffn_call(h, ffn_norm, w_in, w_out, layer, half):
    t = h.shape[0]
    return pl.pallas_call(
        _ffn_kernel,
        grid=(t // TM,),
        in_specs=[
            pl.BlockSpec((TM, D_MODEL), lambda i: (i, 0)),
            _resident((None, None, 1, D_MODEL), (layer, half, 0, 0)),
            _resident((None, None, D_MODEL, 2 * D_FF), (layer, half, 0, 0)),
            _resident((None, None, D_FF, D_MODEL), (layer, half, 0, 0)),
        ],
        out_specs=pl.BlockSpec((TM, D_MODEL), lambda i: (i, 0)),
        out_shape=jax.ShapeDtypeStruct((t, D_MODEL), F32),
        scratch_shapes=[pltpu.VMEM((TM, D_FF), BF16)],
        compiler_params=_params(1),
        name="ffn",
    )(h, ffn_norm, w_in, w_out)


def _mla_proj_kernel(h_ref, pos_ref, g_ref, wd_ref, qn_ref, wuq_ref, kvn_ref, wukv_ref,
                     invf_ref, q_ref, k_ref, v_ref):
    scale = (MLA_NOPE + MLA_ROPE) ** -0.5 * LOG2E
    u = _rms(h_ref[...], g_ref[...]).astype(BF16)
    c = jnp.dot(u, wd_ref[...], preferred_element_type=F32)
    cq = _rms(c[:, :MLA_Q_LORA], qn_ref[...]).astype(BF16)
    ckv = _rms(c[:, MLA_Q_LORA:MLA_Q_LORA + MLA_KV_LORA], kvn_ref[...]).astype(BF16)
    k_rot = c[:, MLA_Q_LORA + MLA_KV_LORA:]

    ang = pos_ref[...].astype(F32) * invf_ref[...]
    cosv = jnp.cos(ang)
    sinv = jnp.sin(ang)
    half = MLA_ROPE // 2
    lane = lax.broadcasted_iota(jnp.int32, (TM // ROPE_PACK, LANES), 1)
    cos_rows, sin_up_rows, sin_dn_rows = [], [], []
    for a in range(ROPE_PACK):
        lo_shift = (-a * half) % LANES
        hi_shift = (half - a * half) % LANES
        cos_lo, cos_hi = _lane_roll(cosv, lo_shift), _lane_roll(cosv, hi_shift)
        sin_lo, sin_hi = _lane_roll(sinv, lo_shift), _lane_roll(sinv, hi_shift)
        cos_rows.append(jnp.where(lane < half, cos_lo, jnp.where(lane < MLA_ROPE, cos_hi, 0.0)))
        sin_up_rows.append(jnp.where((lane >= half) & (lane < MLA_ROPE), sin_hi, 0.0))
        sin_dn_rows.append(jnp.where(lane < half, -sin_lo, 0.0))
    cos_t = jnp.concatenate(cos_rows, axis=0)
    sin_up = jnp.concatenate(sin_up_rows, axis=0)
    sin_dn = jnp.concatenate(sin_dn_rows, axis=0)

    def rope(x):
        return (x * cos_t + pltpu.roll(x, half, 1) * sin_up
                + pltpu.roll(x, LANES - half, 1) * sin_dn)

    k_rope = rope(k_rot).astype(BF16)
    q_lin = jnp.dot(cq, wuq_ref[...], preferred_element_type=F32)
    kv = jnp.dot(ckv, wukv_ref[...], preferred_element_type=F32)
    for hh in range(HEADS):
        q_nope = q_lin[:, hh * LANES:(hh + 1) * LANES]
        q_rot = q_lin[:, (HEADS + hh) * LANES:(HEADS + hh + 1) * LANES]
        q_ref[:, hh * QK_DIM:hh * QK_DIM + LANES] = (q_nope * scale).astype(BF16)
        q_ref[:, hh * QK_DIM + LANES:(hh + 1) * QK_DIM] = (rope(q_rot) * scale).astype(BF16)
        k_ref[:, hh * QK_DIM:hh * QK_DIM + LANES] = kv[:, hh * 256:hh * 256 + 128].astype(BF16)
        k_ref[:, hh * QK_DIM + LANES:(hh + 1) * QK_DIM] = k_rope
        v_ref[:, hh * V_DIM:(hh + 1) * V_DIM] = kv[:, hh * 256 + 128:(hh + 1) * 256].astype(BF16)


def _mla_proj_call(h, pos, mix_norm, wd, qn, wuq, kvn, wukv, invf, layer, j):
    t = h.shape[0]
    tok = lambda w: pl.BlockSpec((TM, w), lambda i: (i, 0))
    return pl.pallas_call(
        _mla_proj_kernel,
        grid=(t // TM,),
        in_specs=[
            tok(D_MODEL),
            pl.BlockSpec((None, TM // ROPE_PACK, LANES), lambda i: (i, 0, 0)),
            _resident((None, 1, D_MODEL), (layer, 0, 0)),
            _resident((None, D_MODEL, 512), (j, 0, 0)),
            _resident((None, 1, MLA_Q_LORA), (j, 0, 0)),
            _resident((None, MLA_Q_LORA, 2 * HEADS * LANES), (j, 0, 0)),
            _resident((None, 1, MLA_KV_LORA), (j, 0, 0)),
            _resident((None, MLA_KV_LORA, HEADS * (MLA_NOPE + MLA_V)), (j, 0, 0)),
            _resident((1, LANES), (0, 0)),
        ],
        out_specs=[tok(HEADS * QK_DIM), tok(HEADS * QK_DIM), tok(HEADS * V_DIM)],
        out_shape=[
            jax.ShapeDtypeStruct((t, HEADS * QK_DIM), BF16),
            jax.ShapeDtypeStruct((t, HEADS * QK_DIM), BF16),
            jax.ShapeDtypeStruct((t, HEADS * V_DIM), BF16),
        ],
        compiler_params=_params(1),
        name="mla_proj",
    )(h, pos, mix_norm, wd, qn, wuq, kvn, wukv, invf)


def _split3(x):
    hi = x.astype(BF16).astype(F32)
    r = x - hi
    mid = r.astype(BF16).astype(F32)
    lo = r - mid
    return hi, mid, lo


def _fox_proj_kernel(h_ref, g_ref, w_ref, bf_ref, q_ref, k_ref, v_ref, carry_ref, tri_ref):
    hd = HEADS * FOX_HEAD_DIM
    scale = FOX_HEAD_DIM ** -0.5 * LOG2E

    @pl.when((pl.program_id(0) == 0) & (pl.program_id(1) == 0))
    def _():
        r = lax.broadcasted_iota(jnp.int32, (TM, TM), 0)
        cc = lax.broadcasted_iota(jnp.int32, (TM, TM), 1)
        tri_ref[...] = (cc <= r).astype(BF16)

    @pl.when(pl.program_id(1) == 0)
    def _():
        carry_ref[...] = jnp.zeros_like(carry_ref)

    u = _rms(h_ref[...], g_ref[...]).astype(BF16)
    q = jnp.dot(u, w_ref[:, :hd], preferred_element_type=F32)
    k = jnp.dot(u, w_ref[:, hd:2 * hd], preferred_element_type=F32)
    v = jnp.dot(u, w_ref[:, 2 * hd:3 * hd], preferred_element_type=F32)
    f_logit = jnp.dot(u, w_ref[:, 3 * hd:], preferred_element_type=F32)
    z = f_logit + bf_ref[...]
    log_f = -(jnp.maximum(-z, 0.0) + jnp.log1p(jnp.exp(-jnp.abs(z))))

    lane = lax.broadcasted_iota(jnp.int32, (TM, LANES), 1)
    hi, mid, lo = _split3(log_f)
    packed = jnp.where(lane < HEADS, hi,
                       jnp.where(lane < 2 * HEADS, _lane_roll(mid, HEADS),
                                 jnp.where(lane < 3 * HEADS, _lane_roll(lo, 2 * HEADS), 0.0)))
    psum = jnp.dot(tri_ref[...], packed.astype(BF16), preferred_element_type=F32)
    cum = ((_lane_roll(psum, -2 * HEADS) + _lane_roll(psum, -HEADS)) + psum) + carry_ref[...]
    carry_ref[...] = cum[TM - 1:TM, :]

    cum2 = cum * LOG2E
    for hh in range(HEADS):
        cb = jnp.broadcast_to(cum2[:, hh:hh + 1], (TM, LANES))
        c_hi, c_mid, c_lo = _split3(cb)
        parts = jnp.where((lane == 0) | (lane == 3), c_hi,
                          jnp.where((lane == 1) | (lane == 4), c_mid, c_lo))
        q_ex = jnp.where(lane < 3, parts, jnp.where(lane < 6, 1.0, 0.0))
        k_ex = jnp.where(lane < 3, 1.0, jnp.where(lane < 6, -parts, 0.0))
        q_ref[:, hh * QK_DIM:hh * QK_DIM + LANES] = (
            q[:, hh * LANES:(hh + 1) * LANES] * scale).astype(BF16)
        q_ref[:, hh * QK_DIM + LANES:(hh + 1) * QK_DIM] = q_ex.astype(BF16)
        k_ref[:, hh * QK_DIM:hh * QK_DIM + LANES] = k[:, hh * LANES:(hh + 1) * LANES].astype(BF16)
        k_ref[:, hh * QK_DIM + LANES:(hh + 1) * QK_DIM] = k_ex.astype(BF16)
    v_ref[...] = v.astype(BF16)


def _fox_proj_call(h, mix_norm, w_in, b_f, layer, j, batch, seq):
    t = h.shape[0]
    nt = seq // TM
    tok = lambda w: pl.BlockSpec((TM, w), lambda b, i: (b * nt + i, 0))
    return pl.pallas_call(
        _fox_proj_kernel,
        grid=(batch, nt),
        in_specs=[
            tok(D_MODEL),
            _resident((None, 1, D_MODEL), (layer, 0, 0)),
            _resident((None, D_MODEL, 3 * HEADS * FOX_HEAD_DIM + LANES), (j, 0, 0)),
            _resident((None, 1, LANES), (j, 0, 0)),
        ],
        out_specs=[tok(HEADS * QK_DIM), tok(HEADS * QK_DIM), tok(HEADS * V_DIM)],
        out_shape=[
            jax.ShapeDtypeStruct((t, HEADS * QK_DIM), BF16),
            jax.ShapeDtypeStruct((t, HEADS * QK_DIM), BF16),
            jax.ShapeDtypeStruct((t, HEADS * V_DIM), BF16),
        ],
        scratch_shapes=[pltpu.VMEM((1, LANES), F32), pltpu.VMEM((TM, TM), BF16)],
        compiler_params=_params(2),
        name="fox_proj",
    )(h, mix_norm, w_in, b_f)


def _attn_kernel(q_ref, k_ref, v_ref, o_ref, vt_ref, *, chunk_causal, seq):
    key = lax.broadcasted_iota(jnp.int32, (TQ, TQ), 0)
    qry = lax.broadcasted_iota(jnp.int32, (TQ, TQ), 1)
    if chunk_causal:
        shift = CHUNK.bit_length() - 1
        allowed = (key >> shift) <= (qry >> shift)
    else:
        allowed = key <= qry
    nt = (((1,), (1,)), ((), ()))

    for hh in range(HEADS_PER_STEP):
        vt_ref[hh, 0:V_DIM, :] = (
            v_ref[:, hh * V_DIM:(hh + 1) * V_DIM].astype(F32).T.astype(BF16))
        vt_ref[hh, V_DIM:, :] = jnp.ones((VT_ROWS - V_DIM, seq), BF16)

    def scores(hh, i):
        q0 = i * TQ
        qk = slice(hh * QK_DIM, (hh + 1) * QK_DIM)
        q = q_ref[q0:q0 + TQ, qk]
        s_dg = lax.dot_general(k_ref[q0:q0 + TQ, qk], q, nt, preferred_element_type=F32)
        s_dg = jnp.where(allowed, s_dg, NEG_INF)
        s_lo = None
        if i > 0:
            s_lo = lax.dot_general(k_ref[0:q0, qk], q, nt, preferred_element_type=F32)
        return s_dg, s_lo

    def finish(hh, i, s_dg, s_lo):
        q0 = i * TQ
        m = jnp.max(s_dg, axis=0, keepdims=True)
        if i > 0:
            m = jnp.maximum(m, jnp.max(s_lo, axis=0, keepdims=True))
        o_t = jnp.dot(vt_ref[hh, :, q0:q0 + TQ], jnp.exp2(s_dg - m).astype(BF16),
                      preferred_element_type=F32)
        if i > 0:
            o_t = o_t + jnp.dot(vt_ref[hh, :, 0:q0], jnp.exp2(s_lo - m).astype(BF16),
                                preferred_element_type=F32)
        out = o_t[0:V_DIM, :] / o_t[V_DIM:V_DIM + 1, :]
        o_ref[q0:q0 + TQ, hh * V_DIM:(hh + 1) * V_DIM] = out.T.astype(o_ref.dtype)

    items = [(hh, i) for hh in range(HEADS_PER_STEP) for i in range(seq // TQ)]
    pending = [scores(*it) for it in items[:SCORE_LOOKAHEAD]]
    for n, item in enumerate(items):
        if n + SCORE_LOOKAHEAD < len(items):
            pending.append(scores(*items[n + SCORE_LOOKAHEAD]))
        finish(*item, *pending.pop(0))


def _attn_call(q, k, v, chunk_causal):
    batch, seq, _ = q.shape
    return pl.pallas_call(
        functools.partial(_attn_kernel, chunk_causal=chunk_causal, seq=seq),
        grid=(batch, HEADS // HEADS_PER_STEP),
        in_specs=[
            pl.BlockSpec((None, seq, HEADS_PER_STEP * QK_DIM), lambda b, h: (b, 0, h)),
            pl.BlockSpec((None, seq, HEADS_PER_STEP * QK_DIM), lambda b, h: (b, 0, h)),
            pl.BlockSpec((None, seq, HEADS_PER_STEP * V_DIM), lambda b, h: (b, 0, h)),
        ],
        out_specs=pl.BlockSpec((None, seq, HEADS_PER_STEP * V_DIM), lambda b, h: (b, 0, h)),
        out_shape=jax.ShapeDtypeStruct((batch, seq, HEADS * V_DIM), BF16),
        scratch_shapes=[pltpu.VMEM((HEADS_PER_STEP, VT_ROWS, seq), BF16)],
        compiler_params=_params(2),
        name="attn",
    )(q, k, v)


def _post_kernel(h_ref, a_ref, p_ref, wo_ref, g_ref, w_in_ref, w_out_ref, pg_ref, wg_ref,
                 wp_ref, fn_ref, o_ref, act_ref, *, final):
    h = h_ref[...] + jnp.dot(a_ref[...], wo_ref[...], preferred_element_type=F32)
    h = _ffn_half_step(h, g_ref[...], w_in_ref, w_out_ref, act_ref)
    gate = _sigmoid(jnp.dot(_rms(h, pg_ref[...]).astype(BF16), wg_ref[...],
                            preferred_element_type=F32))
    emb = jnp.dot(p_ref[...].astype(BF16), wp_ref[...], preferred_element_type=F32)
    h = h + gate * emb
    if final:
        h = _rms(h, fn_ref[...])
    o_ref[...] = h


def _post_call(h, attn, p, w_o, ffn_norm, w_in, w_out, ple_norm, w_gate, w_proj, final_norm,
               layer, j, final):
    t = h.shape[0]
    return pl.pallas_call(
        functools.partial(_post_kernel, final=final),
        grid=(t // TM,),
        in_specs=[
            pl.BlockSpec((TM, D_MODEL), lambda i: (i, 0)),
            pl.BlockSpec((TM, HEADS * V_DIM), lambda i: (i, 0)),
            pl.BlockSpec((None, TM, PLE_DIM), lambda i: (layer, i, 0)),
            _resident((None, HEADS * V_DIM, D_MODEL), (j, 0, 0)),
            _resident((None, None, 1, D_MODEL), (layer, 1, 0, 0)),
            _resident((None, None, D_MODEL, 2 * D_FF), (layer, 1, 0, 0)),
            _resident((None, None, D_FF, D_MODEL), (layer, 1, 0, 0)),
            _resident((None, 1, D_MODEL), (layer, 0, 0)),
            _resident((None, D_MODEL, D_MODEL), (layer, 0, 0)),
            _resident((None, PLE_DIM, D_MODEL), (layer, 0, 0)),
            _resident((1, D_MODEL), (0, 0)),
        ],
        out_specs=pl.BlockSpec((TM, D_MODEL), lambda i: (i, 0)),
        out_shape=jax.ShapeDtypeStruct((t, D_MODEL), F32),
        scratch_shapes=[pltpu.VMEM((TM, D_FF), BF16)],
        compiler_params=_params(1),
        name="post",
    )(h, attn, p, w_o, ffn_norm, w_in, w_out, ple_norm, w_gate, w_proj, final_norm)


def kernel(x, p, positions, ffn_norm, ffn_w_in, ffn_w_out, mix_norm, mla_w_down, mla_q_norm,
           mla_w_uq, mla_kv_norm, mla_w_ukv, mla_w_o, fox_w_in, fox_b_f, fox_w_o, ple_norm,
           ple_w_gate, ple_w_proj, final_norm):
    batch, seq, _ = x.shape
    t = batch * seq
    n_mla = mla_w_down.shape[0]
    n_fox = fox_w_in.shape[0]

    w_in = ffn_w_in.astype(BF16)
    w_out = ffn_w_out.astype(BF16)
    ffn_g = ffn_norm.reshape(DEPTH, 2, 1, D_MODEL)
    mix_g = mix_norm.reshape(DEPTH, 1, D_MODEL)
    ple_g = ple_norm.reshape(DEPTH, 1, D_MODEL)
    fin_g = final_norm.reshape(1, D_MODEL)
    w_gate = ple_w_gate.astype(BF16)
    w_proj = ple_w_proj.astype(BF16)

    wd = jnp.pad(mla_w_down, ((0, 0), (0, 0), (0, LANES - MLA_ROPE))).astype(BF16)
    wq = mla_w_uq.reshape(n_mla, MLA_Q_LORA, HEADS, MLA_NOPE + MLA_ROPE)
    wq_nope = wq[..., :MLA_NOPE].reshape(n_mla, MLA_Q_LORA, HEADS * LANES)
    wq_rope = jnp.pad(wq[..., MLA_NOPE:], ((0, 0), (0, 0), (0, 0), (0, LANES - MLA_ROPE)))
    wuq = jnp.concatenate(
        [wq_nope, wq_rope.reshape(n_mla, MLA_Q_LORA, HEADS * LANES)], axis=-1).astype(BF16)
    wukv = mla_w_ukv.astype(BF16)
    q_g = mla_q_norm.reshape(n_mla, 1, MLA_Q_LORA)
    kv_g = mla_kv_norm.reshape(n_mla, 1, MLA_KV_LORA)
    mla_wo = mla_w_o.astype(BF16)
    inv_freq = ROPE_THETA ** (-jnp.arange(0, MLA_ROPE, 2, dtype=F32) / MLA_ROPE)
    invf = jnp.tile(inv_freq, ROPE_PACK)[None, :]

    fox_w = jnp.pad(fox_w_in, ((0, 0), (0, 0), (0, LANES - HEADS))).astype(BF16)
    fox_b = jnp.pad(fox_b_f, ((0, 0), (0, LANES - HEADS))).reshape(n_fox, 1, LANES)
    fox_wo = fox_w_o.astype(BF16)

    pos = positions.reshape(t // TM, ROPE_PACK, TM // ROPE_PACK).transpose(0, 2, 1)
    pos = jnp.repeat(pos, MLA_ROPE // 2, axis=2)
    p_tok = p.reshape(DEPTH, t, PLE_DIM)

    h = x.reshape(t, D_MODEL)
    for i in range(DEPTH):
        j = i // 2
        h = _ffn_call(h, ffn_g, w_in, w_out, i, 0)
        if i % 2 == 0:
            q, k, v = _mla_proj_call(h, pos, mix_g, wd, q_g, wuq, kv_g, wukv, invf, i, j)
            w_o = mla_wo
        else:
            q, k, v = _fox_proj_call(h, mix_g, fox_w, fox_b, i, j, batch, seq)
            w_o = fox_wo
        attn = _attn_call(q.reshape(batch, seq, -1), k.reshape(batch, seq, -1),
                          v.reshape(batch, seq, -1), chunk_causal=(i % 2 == 0))
        h = _post_call(h, attn.reshape(t, -1), p_tok, w_o, ffn_g, w_in, w_out, ple_g, w_gate,
                       w_proj, fin_g, i, j, final=(i == DEPTH - 1))
    return h.reshape(batch, seq, D_MODEL)
```

```python
import functools

import jax
import jax.numpy as jnp
from jax import lax
from jax.experimental import pallas as pl
from jax.experimental.pallas import tpu as pltpu

D_MODEL = 1024
DEPTH = 4
CHUNK = 64
D_FF = 2816
PLE_DIM = 256
HEADS = 8
MLA_NOPE = 128
MLA_ROPE = 64
MLA_V = 128
MLA_Q_LORA = 256
MLA_KV_LORA = 128
ROPE_THETA = 10000.0
FOX_HEAD_DIM = 128
NORM_EPS = 1e-6
NEG_INF = -1e30

F32 = jnp.float32
BF16 = jnp.bfloat16

LANES = 128
QK_DIM = 2 * LANES
V_DIM = 128
TM = 512
FFN_COLS = 256
TQ = 256
HEADS_PER_STEP = 2
ROW_SPLIT = 2
SCORE_LOOKAHEAD = 3
BF16_SUBLANES = 16
VT_ROWS = V_DIM + BF16_SUBLANES
LOG2E = 1.4426950408889634
ROPE_PACK = LANES // (MLA_ROPE // 2)
VMEM_LIMIT = 56 * 1024 * 1024


def _rms(x, g):
    return x * lax.rsqrt(jnp.mean(x * x, axis=-1, keepdims=True) + NORM_EPS) * g


def _sigmoid(x):
    return 1.0 / (1.0 + jnp.exp(-x))


def _lane_roll(x, shift):
    return x if shift % LANES == 0 else pltpu.roll(x, shift % LANES, 1)


def _ffn_half_step(h, g, w_in_ref, w_out_ref, act_ref):
    xn = _rms(h, g).astype(BF16)
    for c in range(D_FF // FFN_COLS):
        lo = c * FFN_COLS
        gate = jnp.dot(xn, w_in_ref[:, lo:lo + FFN_COLS], preferred_element_type=F32)
        up = jnp.dot(xn, w_in_ref[:, D_FF + lo:D_FF + lo + FFN_COLS],
                     preferred_element_type=F32)
        act_ref[:, lo:lo + FFN_COLS] = (gate * _sigmoid(gate) * up).astype(BF16)
    y = jnp.dot(act_ref[...], w_out_ref[...], preferred_element_type=F32)
    return h + 0.5 * y


def _resident(block_shape, index):
    return pl.BlockSpec(block_shape, lambda *_: index, pipeline_mode=pl.Buffered(1))


def _params(n_axes):
    return pltpu.CompilerParams(
        dimension_semantics=("arbitrary",) * n_axes, vmem_limit_bytes=VMEM_LIMIT)


def _ffn_specs(layer, half):
    return [
        _resident((None, None, 1, D_MODEL), (layer, half, 0, 0)),
        _resident((None, None, D_MODEL, 2 * D_FF), (layer, half, 0, 0)),
        _resident((None, None, D_FF, D_MODEL), (layer, half, 0, 0)),
    ]


def _pre_mla_kernel(h_ref, fg_ref, w_in_ref, w_out_ref, pos_ref, g_ref, wd_ref, qn_ref, wuq_ref,
                    kvn_ref, wukv_ref, invf_ref, ho_ref, q_ref, k_ref, v_ref, act_ref):
    scale = (MLA_NOPE + MLA_ROPE) ** -0.5 * LOG2E
    h = _ffn_half_step(h_ref[...], fg_ref[...], w_in_ref, w_out_ref, act_ref)
    ho_ref[...] = h

    ang = pos_ref[...].astype(F32) * invf_ref[...]
    cosv = jnp.cos(ang)
    sinv = jnp.sin(ang)
    half = MLA_ROPE // 2
    lane = lax.broadcasted_iota(jnp.int32, (TM // ROPE_PACK, LANES), 1)
    cos_rows, sin_up_rows, sin_dn_rows = [], [], []
    for a in range(ROPE_PACK):
        lo_shift = (-a * half) % LANES
        hi_shift = (half - a * half) % LANES
        cos_lo, cos_hi = _lane_roll(cosv, lo_shift), _lane_roll(cosv, hi_shift)
        sin_lo, sin_hi = _lane_roll(sinv, lo_shift), _lane_roll(sinv, hi_shift)
        cos_rows.append(jnp.where(lane < half, cos_lo, jnp.where(lane < MLA_ROPE, cos_hi, 0.0)))
        sin_up_rows.append(jnp.where((lane >= half) & (lane < MLA_ROPE), sin_hi, 0.0))
        sin_dn_rows.append(jnp.where(lane < half, -sin_lo, 0.0))
    cos_t = jnp.concatenate(cos_rows, axis=0)
    sin_up = jnp.concatenate(sin_up_rows, axis=0)
    sin_dn = jnp.concatenate(sin_dn_rows, axis=0)

    rows = [slice(s * (TM // ROW_SPLIT), (s + 1) * (TM // ROW_SPLIT)) for s in range(ROW_SPLIT)]
    cs = [jnp.dot(_rms(h[r], g_ref[...]).astype(BF16), wd_ref[...],
                  preferred_element_type=F32) for r in rows]
    lins = []
    for c in cs:
        cq = _rms(c[:, :MLA_Q_LORA], qn_ref[...]).astype(BF16)
        ckv = _rms(c[:, MLA_Q_LORA:MLA_Q_LORA + MLA_KV_LORA], kvn_ref[...]).astype(BF16)
        lins.append((jnp.dot(cq, wuq_ref[...], preferred_element_type=F32),
                     jnp.dot(ckv, wukv_ref[...], preferred_element_type=F32)))
    for r, c, (q_lin, kv) in zip(rows, cs, lins):
        def rope(x):
            return (x * cos_t[r] + pltpu.roll(x, half, 1) * sin_up[r]
                    + pltpu.roll(x, LANES - half, 1) * sin_dn[r])

        k_rope = rope(c[:, MLA_Q_LORA + MLA_KV_LORA:]).astype(BF16)
        for hh in range(HEADS):
            q_nope = q_lin[:, hh * LANES:(hh + 1) * LANES]
            q_rot = q_lin[:, (HEADS + hh) * LANES:(HEADS + hh + 1) * LANES]
            q_ref[r, hh * QK_DIM:hh * QK_DIM + LANES] = (q_nope * scale).astype(BF16)
            q_ref[r, hh * QK_DIM + LANES:(hh + 1) * QK_DIM] = (rope(q_rot) * scale).astype(BF16)
            k_ref[r, hh * QK_DIM:hh * QK_DIM + LANES] = kv[:, hh * 256:hh * 256 + 128].astype(BF16)
            k_ref[r, hh * QK_DIM + LANES:(hh + 1) * QK_DIM] = k_rope
            v_ref[r, hh * V_DIM:(hh + 1) * V_DIM] = (
                kv[:, hh * 256 + 128:(hh + 1) * 256].astype(BF16))


def _pre_mla_call(h, ffn_norm, w_in, w_out, pos, mix_norm, wd, qn, wuq, kvn, wukv, invf, layer, j):
    t = h.shape[0]
    tok = lambda w: pl.BlockSpec((TM, w), lambda i: (i, 0))
    return pl.pallas_call(
        _pre_mla_kernel,
        grid=(t // TM,),
        in_specs=[
            tok(D_MODEL),
            *_ffn_specs(layer, 0),
            pl.BlockSpec((None, TM // ROPE_PACK, LANES), lambda i: (i, 0, 0)),
            _resident((None, 1, D_MODEL), (layer, 0, 0)),
            _resident((None, D_MODEL, 512), (j, 0, 0)),
            _resident((None, 1, MLA_Q_LORA), (j, 0, 0)),
            _resident((None, MLA_Q_LORA, 2 * HEADS * LANES), (j, 0, 0)),
            _resident((None, 1, MLA_KV_LORA), (j, 0, 0)),
            _resident((None, MLA_KV_LORA, HEADS * (MLA_NOPE + MLA_V)), (j, 0, 0)),
            _resident((1, LANES), (0, 0)),
        ],
        out_specs=[tok(D_MODEL), tok(HEADS * QK_DIM), tok(HEADS * QK_DIM), tok(HEADS * V_DIM)],
        out_shape=[
            jax.ShapeDtypeStruct((t, D_MODEL), F32),
            jax.ShapeDtypeStruct((t, HEADS * QK_DIM), BF16),
            jax.ShapeDtypeStruct((t, HEADS * QK_DIM), BF16),
            jax.ShapeDtypeStruct((t, HEADS * V_DIM), BF16),
        ],
        scratch_shapes=[pltpu.VMEM((TM, D_FF), BF16)],
        compiler_params=_params(1),
        name="pre_mla",
    )(h, ffn_norm, w_in, w_out, pos, mix_norm, wd, qn, wuq, kvn, wukv, invf)


def _split3(x):
    hi = x.astype(BF16).astype(F32)
    r = x - hi
    mid = r.astype(BF16).astype(F32)
    lo = r - mid
    return hi, mid, lo


def _pre_fox_kernel(h_ref, fg_ref, w_in_ref, w_out_ref, g_ref, w_ref, bf_ref,
                    ho_ref, q_ref, k_ref, v_ref, act_ref, carry_ref, tri_ref):
    hd = HEADS * FOX_HEAD_DIM
    scale = FOX_HEAD_DIM ** -0.5 * LOG2E

    @pl.when((pl.program_id(0) == 0) & (pl.program_id(1) == 0))
    def _():
        r = lax.broadcasted_iota(jnp.int32, (TM, TM), 0)
        cc = lax.broadcasted_iota(jnp.int32, (TM, TM), 1)
        tri_ref[...] = (cc <= r).astype(BF16)

    @pl.when(pl.program_id(1) == 0)
    def _():
        carry_ref[...] = jnp.zeros_like(carry_ref)

    h = _ffn_half_step(h_ref[...], fg_ref[...], w_in_ref, w_out_ref, act_ref)
    ho_ref[...] = h

    u = _rms(h, g_ref[...]).astype(BF16)
    f_logit = jnp.dot(u, w_ref[:, 3 * hd:], preferred_element_type=F32)
    q = jnp.dot(u, w_ref[:, :hd], preferred_element_type=F32)
    z = f_logit + bf_ref[...]
    log_f = -(jnp.maximum(-z, 0.0) + jnp.log1p(jnp.exp(-jnp.abs(z))))

    lane = lax.broadcasted_iota(jnp.int32, (TM, LANES), 1)
    hi, mid, lo = _split3(log_f)
    packed = jnp.where(lane < HEADS, hi,
                       jnp.where(lane < 2 * HEADS, _lane_roll(mid, HEADS),
                                 jnp.where(lane < 3 * HEADS, _lane_roll(lo, 2 * HEADS), 0.0)))
    psum = jnp.dot(tri_ref[...], packed.astype(BF16), preferred_element_type=F32)
    cum = ((_lane_roll(psum, -2 * HEADS) + _lane_roll(psum, -HEADS)) + psum) + carry_ref[...]
    carry_ref[...] = cum[TM - 1:TM, :]

    k = jnp.dot(u, w_ref[:, hd:2 * hd], preferred_element_type=F32)

    cum2 = cum * LOG2E
    parts = []
    for hh in range(HEADS):
        cb = jnp.broadcast_to(cum2[:, hh:hh + 1], (TM, LANES))
        c_hi, c_mid, c_lo = _split3(cb)
        parts.append(jnp.where((lane == 0) | (lane == 3), c_hi,
                               jnp.where((lane == 1) | (lane == 4), c_mid, c_lo)))
        q_ex = jnp.where(lane < 3, parts[hh], jnp.where(lane < 6, 1.0, 0.0))
        q_ref[:, hh * QK_DIM:hh * QK_DIM + LANES] = (
            q[:, hh * LANES:(hh + 1) * LANES] * scale).astype(BF16)
        q_ref[:, hh * QK_DIM + LANES:(hh + 1) * QK_DIM] = q_ex.astype(BF16)
    v = jnp.dot(u, w_ref[:, 2 * hd:3 * hd], preferred_element_type=F32)
    for hh in range(HEADS):
        k_ex = jnp.where(lane < 3, 1.0, jnp.where(lane < 6, -parts[hh], 0.0))
        k_ref[:, hh * QK_DIM:hh * QK_DIM + LANES] = k[:, hh * LANES:(hh + 1) * LANES].astype(BF16)
        k_ref[:, hh * QK_DIM + LANES:(hh + 1) * QK_DIM] = k_ex.astype(BF16)
    v_ref[...] = v.astype(BF16)


def _pre_fox_call(h, ffn_norm, w_in, w_out, mix_norm, fox_w, b_f, layer, j, batch, seq):
    t = h.shape[0]
    nt = seq // TM
    tok = lambda w: pl.BlockSpec((TM, w), lambda b, i: (b * nt + i, 0))
    return pl.pallas_call(
        _pre_fox_kernel,
        grid=(batch, nt),
        in_specs=[
            tok(D_MODEL),
            *_ffn_specs(layer, 0),
            _resident((None, 1, D_MODEL), (layer, 0, 0)),
            _resident((None, D_MODEL, 3 * HEADS * FOX_HEAD_DIM + LANES), (j, 0, 0)),
            _resident((None, 1, LANES), (j, 0, 0)),
        ],
        out_specs=[tok(D_MODEL), tok(HEADS * QK_DIM), tok(HEADS * QK_DIM), tok(HEADS * V_DIM)],
        out_shape=[
            jax.ShapeDtypeStruct((t, D_MODEL), F32),
            jax.ShapeDtypeStruct((t, HEADS * QK_DIM), BF16),
            jax.ShapeDtypeStruct((t, HEADS * QK_DIM), BF16),
            jax.ShapeDtypeStruct((t, HEADS * V_DIM), BF16),
        ],
        scratch_shapes=[pltpu.VMEM((TM, D_FF), BF16), pltpu.VMEM((1, LANES), F32),
                        pltpu.VMEM((TM, TM), BF16)],
        compiler_params=_params(2),
        name="pre_fox",
    )(h, ffn_norm, w_in, w_out, mix_norm, fox_w, b_f)


def _attn_kernel(q_ref, k_ref, v_ref, o_ref, vt_ref, *, chunk_causal, seq):
    key = lax.broadcasted_iota(jnp.int32, (TQ, TQ), 0)
    qry = lax.broadcasted_iota(jnp.int32, (TQ, TQ), 1)
    if chunk_causal:
        shift = CHUNK.bit_length() - 1
        allowed = (key >> shift) <= (qry >> shift)
    else:
        allowed = key <= qry
    nt = (((1,), (1,)), ((), ()))

    for hh in range(HEADS_PER_STEP):
        vt_ref[hh, 0:V_DIM, :] = (
            v_ref[:, hh * V_DIM:(hh + 1) * V_DIM].astype(F32).T.astype(BF16))
        vt_ref[hh, V_DIM:, :] = jnp.ones((VT_ROWS - V_DIM, seq), BF16)

    def scores(hh, i):
        q0 = i * TQ
        qk = slice(hh * QK_DIM, (hh + 1) * QK_DIM)
        q = q_ref[q0:q0 + TQ, qk]
        s_dg = lax.dot_general(k_ref[q0:q0 + TQ, qk], q, nt, preferred_element_type=F32)
        s_dg = jnp.where(allowed, s_dg, NEG_INF)
        s_lo = None
        if i > 0:
            s_lo = lax.dot_general(k_ref[0:q0, qk], q, nt, preferred_element_type=F32)
        return s_dg, s_lo

    def finish(hh, i, s_dg, s_lo):
        q0 = i * TQ
        m = jnp.max(s_dg, axis=0, keepdims=True)
        if i > 0:
            m = jnp.maximum(m, jnp.max(s_lo, axis=0, keepdims=True))
        o_t = jnp.dot(vt_ref[hh, :, q0:q0 + TQ], jnp.exp2(s_dg - m).astype(BF16),
                      preferred_element_type=F32)
        if i > 0:
            o_t = o_t + jnp.dot(vt_ref[hh, :, 0:q0], jnp.exp2(s_lo - m).astype(BF16),
                                preferred_element_type=F32)
        out = o_t[0:V_DIM, :] / o_t[V_DIM:V_DIM + 1, :]
        o_ref[q0:q0 + TQ, hh * V_DIM:(hh + 1) * V_DIM] = out.T.astype(o_ref.dtype)

    items = [(hh, i) for hh in range(HEADS_PER_STEP) for i in range(seq // TQ)]
    pending = [scores(*it) for it in items[:SCORE_LOOKAHEAD]]
    for n, item in enumerate(items):
        if n + SCORE_LOOKAHEAD < len(items):
            pending.append(scores(*items[n + SCORE_LOOKAHEAD]))
        finish(*item, *pending.pop(0))


def _attn_call(q, k, v, chunk_causal):
    batch, seq, _ = q.shape
    return pl.pallas_call(
        functools.partial(_attn_kernel, chunk_causal=chunk_causal, seq=seq),
        grid=(batch, HEADS // HEADS_PER_STEP),
        in_specs=[
            pl.BlockSpec((None, seq, HEADS_PER_STEP * QK_DIM), lambda b, h: (b, 0, h)),
            pl.BlockSpec((None, seq, HEADS_PER_STEP * QK_DIM), lambda b, h: (b, 0, h)),
            pl.BlockSpec((None, seq, HEADS_PER_STEP * V_DIM), lambda b, h: (b, 0, h)),
        ],
        out_specs=pl.BlockSpec((None, seq, HEADS_PER_STEP * V_DIM), lambda b, h: (b, 0, h)),
        out_shape=jax.ShapeDtypeStruct((batch, seq, HEADS * V_DIM), BF16),
        scratch_shapes=[pltpu.VMEM((HEADS_PER_STEP, VT_ROWS, seq), BF16)],
        compiler_params=_params(2),
        name="attn",
    )(q, k, v)


def _post_kernel(h_ref, a_ref, p_ref, wo_ref, g_ref, w_in_ref, w_out_ref, pg_ref, wg_ref,
                 wp_ref, fn_ref, o_ref, act_ref, *, final):
    h = h_ref[...] + jnp.dot(a_ref[...], wo_ref[...], preferred_element_type=F32)
    h = _ffn_half_step(h, g_ref[...], w_in_ref, w_out_ref, act_ref)
    gate = _sigmoid(jnp.dot(_rms(h, pg_ref[...]).astype(BF16), wg_ref[...],
                            preferred_element_type=F32))
    emb = jnp.dot(p_ref[...].astype(BF16), wp_ref[...], preferred_element_type=F32)
    h = h + gate * emb
    if final:
        h = _rms(h, fn_ref[...])
    o_ref[...] = h


def _post_call(h, attn, p, w_o, ffn_norm, w_in, w_out, ple_norm, w_gate, w_proj, final_norm,
               layer, j, final):
    t = h.shape[0]
    return pl.pallas_call(
        functools.partial(_post_kernel, final=final),
        grid=(t // TM,),
        in_specs=[
            pl.BlockSpec((TM, D_MODEL), lambda i: (i, 0)),
            pl.BlockSpec((TM, HEADS * V_DIM), lambda i: (i, 0)),
            pl.BlockSpec((None, TM, PLE_DIM), lambda i: (layer, i, 0)),
            _resident((None, HEADS * V_DIM, D_MODEL), (j, 0, 0)),
            _resident((None, None, 1, D_MODEL), (layer, 1, 0, 0)),
            _resident((None, None, D_MODEL, 2 * D_FF), (layer, 1, 0, 0)),
            _resident((None, None, D_FF, D_MODEL), (layer, 1, 0, 0)),
            _resident((None, 1, D_MODEL), (layer, 0, 0)),
            _resident((None, D_MODEL, D_MODEL), (layer, 0, 0)),
            _resident((None, PLE_DIM, D_MODEL), (layer, 0, 0)),
            _resident((1, D_MODEL), (0, 0)),
        ],
        out_specs=pl.BlockSpec((TM, D_MODEL), lambda i: (i, 0)),
        out_shape=jax.ShapeDtypeStruct((t, D_MODEL), F32),
        scratch_shapes=[pltpu.VMEM((TM, D_FF), BF16)],
        compiler_params=_params(1),
        name="post",
    )(h, attn, p, w_o, ffn_norm, w_in, w_out, ple_norm, w_gate, w_proj, final_norm)


def kernel(x, p, positions, ffn_norm, ffn_w_in, ffn_w_out, mix_norm, mla_w_down, mla_q_norm,
           mla_w_uq, mla_kv_norm, mla_w_ukv, mla_w_o, fox_w_in, fox_b_f, fox_w_o, ple_norm,
           ple_w_gate, ple_w_proj, final_norm):
    batch, seq, _ = x.shape
    t = batch * seq
    n_mla = mla_w_down.shape[0]
    n_fox = fox_w_in.shape[0]

    w_in = ffn_w_in.astype(BF16)
    w_out = ffn_w_out.astype(BF16)
    ffn_g = ffn_norm.reshape(DEPTH, 2, 1, D_MODEL)
    mix_g = mix_norm.reshape(DEPTH, 1, D_MODEL)
    ple_g = ple_norm.reshape(DEPTH, 1, D_MODEL)
    fin_g = final_norm.reshape(1, D_MODEL)
    w_gate = ple_w_gate.astype(BF16)
    w_proj = ple_w_proj.astype(BF16)

    wd = jnp.pad(mla_w_down, ((0, 0), (0, 0), (0, LANES - MLA_ROPE))).astype(BF16)
    wq = mla_w_uq.reshape(n_mla, MLA_Q_LORA, HEADS, MLA_NOPE + MLA_ROPE)
    wq_nope = wq[..., :MLA_NOPE].reshape(n_mla, MLA_Q_LORA, HEADS * LANES)
    wq_rope = jnp.pad(wq[..., MLA_NOPE:], ((0, 0), (0, 0), (0, 0), (0, LANES - MLA_ROPE)))
    wuq = jnp.concatenate(
        [wq_nope, wq_rope.reshape(n_mla, MLA_Q_LORA, HEADS * LANES)], axis=-1).astype(BF16)
    wukv = mla_w_ukv.astype(BF16)
    q_g = mla_q_norm.reshape(n_mla, 1, MLA_Q_LORA)
    kv_g = mla_kv_norm.reshape(n_mla, 1, MLA_KV_LORA)
    mla_wo = mla_w_o.astype(BF16)
    inv_freq = ROPE_THETA ** (-jnp.arange(0, MLA_ROPE, 2, dtype=F32) / MLA_ROPE)
    invf = jnp.tile(inv_freq, ROPE_PACK)[None, :]

    fox_w = jnp.pad(fox_w_in, ((0, 0), (0, 0), (0, LANES - HEADS))).astype(BF16)
    fox_b = jnp.pad(fox_b_f, ((0, 0), (0, LANES - HEADS))).reshape(n_fox, 1, LANES)
    fox_wo = fox_w_o.astype(BF16)

    pos = positions.reshape(t // TM, ROPE_PACK, TM // ROPE_PACK).transpose(0, 2, 1)
    pos = jnp.repeat(pos, MLA_ROPE // 2, axis=2)
    p_tok = p.reshape(DEPTH, t, PLE_DIM)

    h = x.reshape(t, D_MODEL)
    for i in range(DEPTH):
        j = i // 2
        if i % 2 == 0:
            h, q, k, v = _pre_mla_call(h, ffn_g, w_in, w_out, pos, mix_g, wd, q_g, wuq, kv_g,
                                       wukv, invf, i, j)
            w_o = mla_wo
        else:
            h, q, k, v = _pre_fox_call(h, ffn_g, w_in, w_out, mix_g, fox_w, fox_b, i, j,
                                       batch, seq)
            w_o = fox_wo
        attn = _attn_call(q.reshape(batch, seq, -1), k.reshape(batch, seq, -1),
                          v.reshape(batch, seq, -1), chunk_causal=(i % 2 == 0))
        h = _post_call(h, attn.reshape(t, -1), p_tok, w_o, ffn_g, w_in, w_out, ple_g, w_gate,
                       w_proj, fin_g, i, j, final=(i == DEPTH - 1))
    return h.reshape(batch, seq, D_MODEL)
```

```python
import functools

import jax
import jax.numpy as jnp
from jax import lax
from jax.experimental import pallas as pl
from jax.experimental.pallas import tpu as pltpu

D_MODEL = 1024
DEPTH = 4
CHUNK = 64
D_FF = 2816
PLE_DIM = 256
HEADS = 8
MLA_NOPE = 128
MLA_ROPE = 64
MLA_V = 128
MLA_Q_LORA = 256
MLA_KV_LORA = 128
ROPE_THETA = 10000.0
FOX_HEAD_DIM = 128
NORM_EPS = 1e-6
NEG_INF = -1e30

F32 = jnp.float32
BF16 = jnp.bfloat16

LANES = 128
QK_DIM = 2 * LANES
V_DIM = 128
TM = 512
FFN_COLS = 256
W_IN_ROWS = 64
W_OUT_ROWS = 256
TQ = 256
HEADS_PER_STEP = 4
ROW_SPLIT = 2
SCORE_LOOKAHEAD = 3
BF16_SUBLANES = 16
VT_ROWS = V_DIM + BF16_SUBLANES
LOG2E = 1.4426950408889634
ROPE_PACK = LANES // (MLA_ROPE // 2)
VMEM_LIMIT = 56 * 1024 * 1024


def _rms(x, g):
    return x * lax.rsqrt(jnp.mean(x * x, axis=-1, keepdims=True) + NORM_EPS) * g


def _sigmoid(x):
    return 1.0 / (1.0 + jnp.exp(-x))


def _lane_roll(x, shift):
    return x if shift % LANES == 0 else pltpu.roll(x, shift % LANES, 1)


def _ffn_scratch():
    return [
        pltpu.VMEM((TM, D_FF), BF16),
        pltpu.VMEM((D_MODEL, 2 * D_FF), BF16),
        pltpu.VMEM((D_FF, D_MODEL), BF16),
        pltpu.VMEM((2, W_IN_ROWS, 2 * D_FF), F32),
        pltpu.VMEM((2, W_OUT_ROWS, D_MODEL), F32),
        pltpu.SemaphoreType.DMA((2,)),
        pltpu.SemaphoreType.DMA((2,)),
    ]


def _stage_weight(src, dst_ref, stage_ref, sem, rows):
    n = src.shape[0] // rows

    def copy(c):
        return pltpu.make_async_copy(src.at[pl.ds(c * rows, rows)], stage_ref.at[c % 2],
                                     sem.at[c % 2])

    for c in range(min(2, n)):
        copy(c).start()
    for c in range(n):
        copy(c).wait()
        dst_ref[c * rows:(c + 1) * rows, :] = stage_ref[c % 2].astype(BF16)
        if c + 2 < n:
            copy(c + 2).start()


def _load_ffn_weights(w_in_hbm, w_out_hbm, w_in_ref, w_out_ref, stage_in, stage_out,
                      sem_in, sem_out):
    _stage_weight(w_in_hbm, w_in_ref, stage_in, sem_in, W_IN_ROWS)
    _stage_weight(w_out_hbm, w_out_ref, stage_out, sem_out, W_OUT_ROWS)


def _ffn_half_step(h, g, w_in_ref, w_out_ref, act_ref):
    xn = _rms(h, g).astype(BF16)
    for c in range(D_FF // FFN_COLS):
        lo = c * FFN_COLS
        gate = jnp.dot(xn, w_in_ref[:, lo:lo + FFN_COLS], preferred_element_type=F32)
        up = jnp.dot(xn, w_in_ref[:, D_FF + lo:D_FF + lo + FFN_COLS],
                     preferred_element_type=F32)
        act_ref[:, lo:lo + FFN_COLS] = (gate * _sigmoid(gate) * up).astype(BF16)
    y = jnp.dot(act_ref[...], w_out_ref[...], preferred_element_type=F32)
    return h + 0.5 * y


def _resident(block_shape, index):
    return pl.BlockSpec(block_shape, lambda *_: index, pipeline_mode=pl.Buffered(1))


def _params(n_axes):
    return pltpu.CompilerParams(
        dimension_semantics=("arbitrary",) * n_axes, vmem_limit_bytes=VMEM_LIMIT)


def _ffn_specs(layer, half):
    return [
        _resident((None, None, 1, D_MODEL), (layer, half, 0, 0)),
        pl.BlockSpec(memory_space=pl.ANY),
        pl.BlockSpec(memory_space=pl.ANY),
    ]


def _pre_mla_kernel(h_ref, fg_ref, w_in_hbm, w_out_hbm, pos_ref, g_ref, wd_ref, qn_ref, wuq_ref,
                    kvn_ref, wukv_ref, invf_ref, ho_ref, q_ref, k_ref, v_ref,
                    act_ref, w_in_ref, w_out_ref, *stage, layer):
    scale = (MLA_NOPE + MLA_ROPE) ** -0.5 * LOG2E

    @pl.when(pl.program_id(0) == 0)
    def _():
        _load_ffn_weights(w_in_hbm.at[layer, 0], w_out_hbm.at[layer, 0], w_in_ref, w_out_ref,
                          *stage)

    h = _ffn_half_step(h_ref[...], fg_ref[...], w_in_ref, w_out_ref, act_ref)
    ho_ref[...] = h

    ang = pos_ref[...].astype(F32) * invf_ref[...]
    cosv = jnp.cos(ang)
    sinv = jnp.sin(ang)
    half = MLA_ROPE // 2
    lane = lax.broadcasted_iota(jnp.int32, (TM // ROPE_PACK, LANES), 1)
    cos_rows, sin_up_rows, sin_dn_rows = [], [], []
    for a in range(ROPE_PACK):
        lo_shift = (-a * half) % LANES
        hi_shift = (half - a * half) % LANES
        cos_lo, cos_hi = _lane_roll(cosv, lo_shift), _lane_roll(cosv, hi_shift)
        sin_lo, sin_hi = _lane_roll(sinv, lo_shift), _lane_roll(sinv, hi_shift)
        cos_rows.append(jnp.where(lane < half, cos_lo, jnp.where(lane < MLA_ROPE, cos_hi, 0.0)))
        sin_up_rows.append(jnp.where((lane >= half) & (lane < MLA_ROPE), sin_hi, 0.0))
        sin_dn_rows.append(jnp.where(lane < half, -sin_lo, 0.0))
    cos_t = jnp.concatenate(cos_rows, axis=0)
    sin_up = jnp.concatenate(sin_up_rows, axis=0)
    sin_dn = jnp.concatenate(sin_dn_rows, axis=0)

    rows = [slice(s * (TM // ROW_SPLIT), (s + 1) * (TM // ROW_SPLIT)) for s in range(ROW_SPLIT)]
    cs = [jnp.dot(_rms(h[r], g_ref[...]).astype(BF16), wd_ref[...],
                  preferred_element_type=F32) for r in rows]
    lins = []
    for c in cs:
        cq = _rms(c[:, :MLA_Q_LORA], qn_ref[...]).astype(BF16)
        ckv = _rms(c[:, MLA_Q_LORA:MLA_Q_LORA + MLA_KV_LORA], kvn_ref[...]).astype(BF16)
        lins.append((jnp.dot(cq, wuq_ref[...], preferred_element_type=F32),
                     jnp.dot(ckv, wukv_ref[...], preferred_element_type=F32)))
    for r, c, (q_lin, kv) in zip(rows, cs, lins):
        def rope(x):
            return (x * cos_t[r] + pltpu.roll(x, half, 1) * sin_up[r]
                    + pltpu.roll(x, LANES - half, 1) * sin_dn[r])

        k_rope = rope(c[:, MLA_Q_LORA + MLA_KV_LORA:]).astype(BF16)
        for hh in range(HEADS):
            q_nope = q_lin[:, hh * LANES:(hh + 1) * LANES]
            q_rot = q_lin[:, (HEADS + hh) * LANES:(HEADS + hh + 1) * LANES]
            q_ref[r, hh * QK_DIM:hh * QK_DIM + LANES] = (q_nope * scale).astype(BF16)
            q_ref[r, hh * QK_DIM + LANES:(hh + 1) * QK_DIM] = (rope(q_rot) * scale).astype(BF16)
            k_ref[r, hh * QK_DIM:hh * QK_DIM + LANES] = kv[:, hh * 256:hh * 256 + 128].astype(BF16)
            k_ref[r, hh * QK_DIM + LANES:(hh + 1) * QK_DIM] = k_rope
            v_ref[r, hh * V_DIM:(hh + 1) * V_DIM] = (
                kv[:, hh * 256 + 128:(hh + 1) * 256].astype(BF16))


def _pre_mla_call(h, ffn_norm, w_in, w_out, pos, mix_norm, wd, qn, wuq, kvn, wukv, invf, layer, j):
    t = h.shape[0]
    tok = lambda w: pl.BlockSpec((TM, w), lambda i: (i, 0))
    return pl.pallas_call(
        functools.partial(_pre_mla_kernel, layer=layer),
        grid=(t // TM,),
        in_specs=[
            tok(D_MODEL),
            *_ffn_specs(layer, 0),
            pl.BlockSpec((None, TM // ROPE_PACK, LANES), lambda i: (i, 0, 0)),
            _resident((None, 1, D_MODEL), (layer, 0, 0)),
            _resident((None, D_MODEL, 512), (j, 0, 0)),
            _resident((None, 1, MLA_Q_LORA), (j, 0, 0)),
            _resident((None, MLA_Q_LORA, 2 * HEADS * LANES), (j, 0, 0)),
            _resident((None, 1, MLA_KV_LORA), (j, 0, 0)),
            _resident((None, MLA_KV_LORA, HEADS * (MLA_NOPE + MLA_V)), (j, 0, 0)),
            _resident((1, LANES), (0, 0)),
        ],
        out_specs=[tok(D_MODEL), tok(HEADS * QK_DIM), tok(HEADS * QK_DIM), tok(HEADS * V_DIM)],
        out_shape=[
            jax.ShapeDtypeStruct((t, D_MODEL), F32),
            jax.ShapeDtypeStruct((t, HEADS * QK_DIM), BF16),
            jax.ShapeDtypeStruct((t, HEADS * QK_DIM), BF16),
            jax.ShapeDtypeStruct((t, HEADS * V_DIM), BF16),
        ],
        scratch_shapes=_ffn_scratch(),
        compiler_params=_params(1),
        name="pre_mla",
    )(h, ffn_norm, w_in, w_out, pos, mix_norm, wd, qn, wuq, kvn, wukv, invf)


def _split3(x):
    hi = x.astype(BF16).astype(F32)
    r = x - hi
    mid = r.astype(BF16).astype(F32)
    lo = r - mid
    return hi, mid, lo


def _pre_fox_kernel(h_ref, fg_ref, w_in_hbm, w_out_hbm, g_ref, w_ref, bf_ref,
                    ho_ref, q_ref, k_ref, v_ref, carry_ref, tri_ref,
                    act_ref, w_in_ref, w_out_ref, *stage, layer):
    hd = HEADS * FOX_HEAD_DIM
    scale = FOX_HEAD_DIM ** -0.5 * LOG2E

    @pl.when((pl.program_id(0) == 0) & (pl.program_id(1) == 0))
    def _():
        _load_ffn_weights(w_in_hbm.at[layer, 0], w_out_hbm.at[layer, 0], w_in_ref, w_out_ref,
                          *stage)
        r = lax.broadcasted_iota(jnp.int32, (TM, TM), 0)
        cc = lax.broadcasted_iota(jnp.int32, (TM, TM), 1)
        tri_ref[...] = (cc <= r).astype(BF16)

    @pl.when(pl.program_id(1) == 0)
    def _():
        carry_ref[...] = jnp.zeros_like(carry_ref)

    h = _ffn_half_step(h_ref[...], fg_ref[...], w_in_ref, w_out_ref, act_ref)
    ho_ref[...] = h

    u = _rms(h, g_ref[...]).astype(BF16)
    f_logit = jnp.dot(u, w_ref[:, 3 * hd:], preferred_element_type=F32)
    q = jnp.dot(u, w_ref[:, :hd], preferred_element_type=F32)
    z = f_logit + bf_ref[...]
    log_f = -(jnp.maximum(-z, 0.0) + jnp.log1p(jnp.exp(-jnp.abs(z))))

    lane = lax.broadcasted_iota(jnp.int32, (TM, LANES), 1)
    hi, mid, lo = _split3(log_f)
    packed = jnp.where(lane < HEADS, hi,
                       jnp.where(lane < 2 * HEADS, _lane_roll(mid, HEADS),
                                 jnp.where(lane < 3 * HEADS, _lane_roll(lo, 2 * HEADS), 0.0)))
    psum = jnp.dot(tri_ref[...], packed.astype(BF16), preferred_element_type=F32)
    cum = ((_lane_roll(psum, -2 * HEADS) + _lane_roll(psum, -HEADS)) + psum) + carry_ref[...]
    carry_ref[...] = cum[TM - 1:TM, :]

    k = jnp.dot(u, w_ref[:, hd:2 * hd], preferred_element_type=F32)

    cum2 = cum * LOG2E
    parts = []
    for hh in range(HEADS):
        cb = jnp.broadcast_to(cum2[:, hh:hh + 1], (TM, LANES))
        c_hi, c_mid, c_lo = _split3(cb)
        parts.append(jnp.where((lane == 0) | (lane == 3), c_hi,
                               jnp.where((lane == 1) | (lane == 4), c_mid, c_lo)))
        q_ex = jnp.where(lane < 3, parts[hh], jnp.where(lane < 6, 1.0, 0.0))
        q_ref[:, hh * QK_DIM:hh * QK_DIM + LANES] = (
            q[:, hh * LANES:(hh + 1) * LANES] * scale).astype(BF16)
        q_ref[:, hh * QK_DIM + LANES:(hh + 1) * QK_DIM] = q_ex.astype(BF16)
    v = jnp.dot(u, w_ref[:, 2 * hd:3 * hd], preferred_element_type=F32)
    for hh in range(HEADS):
        k_ex = jnp.where(lane < 3, 1.0, jnp.where(lane < 6, -parts[hh], 0.0))
        k_ref[:, hh * QK_DIM:hh * QK_DIM + LANES] = k[:, hh * LANES:(hh + 1) * LANES].astype(BF16)
        k_ref[:, hh * QK_DIM + LANES:(hh + 1) * QK_DIM] = k_ex.astype(BF16)
    v_ref[...] = v.astype(BF16)


def _pre_fox_call(h, ffn_norm, w_in, w_out, mix_norm, fox_w, b_f, layer, j, batch, seq):
    t = h.shape[0]
    nt = seq // TM
    tok = lambda w: pl.BlockSpec((TM, w), lambda b, i: (b * nt + i, 0))
    return pl.pallas_call(
        functools.partial(_pre_fox_kernel, layer=layer),
        grid=(batch, nt),
        in_specs=[
            tok(D_MODEL),
            *_ffn_specs(layer, 0),
            _resident((None, 1, D_MODEL), (layer, 0, 0)),
            _resident((None, D_MODEL, 3 * HEADS * FOX_HEAD_DIM + LANES), (j, 0, 0)),
            _resident((None, 1, LANES), (j, 0, 0)),
        ],
        out_specs=[tok(D_MODEL), tok(HEADS * QK_DIM), tok(HEADS * QK_DIM), tok(HEADS * V_DIM)],
        out_shape=[
            jax.ShapeDtypeStruct((t, D_MODEL), F32),
            jax.ShapeDtypeStruct((t, HEADS * QK_DIM), BF16),
            jax.ShapeDtypeStruct((t, HEADS * QK_DIM), BF16),
            jax.ShapeDtypeStruct((t, HEADS * V_DIM), BF16),
        ],
        scratch_shapes=[pltpu.VMEM((1, LANES), F32), pltpu.VMEM((TM, TM), BF16),
                        *_ffn_scratch()],
        compiler_params=_params(2),
        name="pre_fox",
    )(h, ffn_norm, w_in, w_out, mix_norm, fox_w, b_f)


def _attn_kernel(q_ref, k_ref, v_ref, o_ref, vt_ref, *, chunk_causal, seq):
    key = lax.broadcasted_iota(jnp.int32, (TQ, TQ), 0)
    qry = lax.broadcasted_iota(jnp.int32, (TQ, TQ), 1)
    if chunk_causal:
        shift = CHUNK.bit_length() - 1
        allowed = (key >> shift) <= (qry >> shift)
    else:
        allowed = key <= qry
    nt = (((1,), (1,)), ((), ()))

    for hh in range(HEADS_PER_STEP):
        vt_ref[hh, 0:V_DIM, :] = (
            v_ref[:, hh * V_DIM:(hh + 1) * V_DIM].astype(F32).T.astype(BF16))
        vt_ref[hh, V_DIM:, :] = jnp.ones((VT_ROWS - V_DIM, seq), BF16)

    def scores(hh, i):
        q0 = i * TQ
        qk = slice(hh * QK_DIM, (hh + 1) * QK_DIM)
        q = q_ref[q0:q0 + TQ, qk]
        s_dg = lax.dot_general(k_ref[q0:q0 + TQ, qk], q, nt, preferred_element_type=F32)
        s_dg = jnp.where(allowed, s_dg, NEG_INF)
        s_lo = None
        if i > 0:
            s_lo = lax.dot_general(k_ref[0:q0, qk], q, nt, preferred_element_type=F32)
        return s_dg, s_lo

    def finish(hh, i, s_dg, s_lo):
        q0 = i * TQ
        m = jnp.max(s_dg, axis=0, keepdims=True)
        if i > 0:
            m = jnp.maximum(m, jnp.max(s_lo, axis=0, keepdims=True))
        o_t = jnp.dot(vt_ref[hh, :, q0:q0 + TQ], jnp.exp2(s_dg - m).astype(BF16),
                      preferred_element_type=F32)
        if i > 0:
            o_t = o_t + jnp.dot(vt_ref[hh, :, 0:q0], jnp.exp2(s_lo - m).astype(BF16),
                                preferred_element_type=F32)
        out = o_t[0:V_DIM, :] / o_t[V_DIM:V_DIM + 1, :]
        o_ref[q0:q0 + TQ, hh * V_DIM:(hh + 1) * V_DIM] = out.T.astype(o_ref.dtype)

    items = [(hh, i) for hh in range(HEADS_PER_STEP) for i in range(seq // TQ)]
    pending = [scores(*it) for it in items[:SCORE_LOOKAHEAD]]
    for n, item in enumerate(items):
        if n + SCORE_LOOKAHEAD < len(items):
            pending.append(scores(*items[n + SCORE_LOOKAHEAD]))
        finish(*item, *pending.pop(0))


def _attn_call(q, k, v, chunk_causal):
    batch, seq, _ = q.shape
    return pl.pallas_call(
        functools.partial(_attn_kernel, chunk_causal=chunk_causal, seq=seq),
        grid=(batch, HEADS // HEADS_PER_STEP),
        in_specs=[
            pl.BlockSpec((None, seq, HEADS_PER_STEP * QK_DIM), lambda b, h: (b, 0, h)),
            pl.BlockSpec((None, seq, HEADS_PER_STEP * QK_DIM), lambda b, h: (b, 0, h)),
            pl.BlockSpec((None, seq, HEADS_PER_STEP * V_DIM), lambda b, h: (b, 0, h)),
        ],
        out_specs=pl.BlockSpec((None, seq, HEADS_PER_STEP * V_DIM), lambda b, h: (b, 0, h)),
        out_shape=jax.ShapeDtypeStruct((batch, seq, HEADS * V_DIM), BF16),
        scratch_shapes=[pltpu.VMEM((HEADS_PER_STEP, VT_ROWS, seq), BF16)],
        compiler_params=_params(2),
        name="attn",
    )(q, k, v)


def _post_kernel(h_ref, a_ref, p_ref, wo_ref, g_ref, w_in_hbm, w_out_hbm, pg_ref, wg_ref,
                 wp_ref, fn_ref, o_ref, act_ref, w_in_ref, w_out_ref, *stage, layer, final):
    @pl.when(pl.program_id(0) == 0)
    def _():
        _load_ffn_weights(w_in_hbm.at[layer, 1], w_out_hbm.at[layer, 1], w_in_ref, w_out_ref,
                          *stage)

    h = h_ref[...] + jnp.dot(a_ref[...], wo_ref[...], preferred_element_type=F32)
    h = _ffn_half_step(h, g_ref[...], w_in_ref, w_out_ref, act_ref)
    gate = _sigmoid(jnp.dot(_rms(h, pg_ref[...]).astype(BF16), wg_ref[...],
                            preferred_element_type=F32))
    emb = jnp.dot(p_ref[...].astype(BF16), wp_ref[...], preferred_element_type=F32)
    h = h + gate * emb
    if final:
        h = _rms(h, fn_ref[...])
    o_ref[...] = h


def _post_call(h, attn, p, w_o, ffn_norm, w_in, w_out, ple_norm, w_gate, w_proj, final_norm,
               layer, j, final):
    t = h.shape[0]
    return pl.pallas_call(
        functools.partial(_post_kernel, layer=layer, final=final),
        grid=(t // TM,),
        in_specs=[
            pl.BlockSpec((TM, D_MODEL), lambda i: (i, 0)),
            pl.BlockSpec((TM, HEADS * V_DIM), lambda i: (i, 0)),
            pl.BlockSpec((None, TM, PLE_DIM), lambda i: (layer, i, 0)),
            _resident((None, HEADS * V_DIM, D_MODEL), (j, 0, 0)),
            *_ffn_specs(layer, 1),
            _resident((None, 1, D_MODEL), (layer, 0, 0)),
            _resident((None, D_MODEL, D_MODEL), (layer, 0, 0)),
            _resident((None, PLE_DIM, D_MODEL), (layer, 0, 0)),
            _resident((1, D_MODEL), (0, 0)),
        ],
        out_specs=pl.BlockSpec((TM, D_MODEL), lambda i: (i, 0)),
        out_shape=jax.ShapeDtypeStruct((t, D_MODEL), F32),
        scratch_shapes=_ffn_scratch(),
        compiler_params=_params(1),
        name="post",
    )(h, attn, p, w_o, ffn_norm, w_in, w_out, ple_norm, w_gate, w_proj, final_norm)


def kernel(x, p, positions, ffn_norm, ffn_w_in, ffn_w_out, mix_norm, mla_w_down, mla_q_norm,
           mla_w_uq, mla_kv_norm, mla_w_ukv, mla_w_o, fox_w_in, fox_b_f, fox_w_o, ple_norm,
           ple_w_gate, ple_w_proj, final_norm):
    batch, seq, _ = x.shape
    t = batch * seq
    n_mla = mla_w_down.shape[0]
    n_fox = fox_w_in.shape[0]

    w_in, w_out = ffn_w_in, ffn_w_out
    ffn_g = ffn_norm.reshape(DEPTH, 2, 1, D_MODEL)
    mix_g = mix_norm.reshape(DEPTH, 1, D_MODEL)
    ple_g = ple_norm.reshape(DEPTH, 1, D_MODEL)
    fin_g = final_norm.reshape(1, D_MODEL)
    w_gate = ple_w_gate.astype(BF16)
    w_proj = ple_w_proj.astype(BF16)

    wd = jnp.pad(mla_w_down, ((0, 0), (0, 0), (0, LANES - MLA_ROPE))).astype(BF16)
    wq = mla_w_uq.reshape(n_mla, MLA_Q_LORA, HEADS, MLA_NOPE + MLA_ROPE)
    wq_nope = wq[..., :MLA_NOPE].reshape(n_mla, MLA_Q_LORA, HEADS * LANES)
    wq_rope = jnp.pad(wq[..., MLA_NOPE:], ((0, 0), (0, 0), (0, 0), (0, LANES - MLA_ROPE)))
    wuq = jnp.concatenate(
        [wq_nope, wq_rope.reshape(n_mla, MLA_Q_LORA, HEADS * LANES)], axis=-1).astype(BF16)
    wukv = mla_w_ukv.astype(BF16)
    q_g = mla_q_norm.reshape(n_mla, 1, MLA_Q_LORA)
    kv_g = mla_kv_norm.reshape(n_mla, 1, MLA_KV_LORA)
    mla_wo = mla_w_o.astype(BF16)
    inv_freq = ROPE_THETA ** (-jnp.arange(0, MLA_ROPE, 2, dtype=F32) / MLA_ROPE)
    invf = jnp.tile(inv_freq, ROPE_PACK)[None, :]

    fox_w = jnp.pad(fox_w_in, ((0, 0), (0, 0), (0, LANES - HEADS))).astype(BF16)
    fox_b = jnp.pad(fox_b_f, ((0, 0), (0, LANES - HEADS))).reshape(n_fox, 1, LANES)
    fox_wo = fox_w_o.astype(BF16)

    pos = positions.reshape(t // TM, ROPE_PACK, TM // ROPE_PACK).transpose(0, 2, 1)
    pos = jnp.repeat(pos, MLA_ROPE // 2, axis=2)
    p_tok = p.reshape(DEPTH, t, PLE_DIM)

    h = x.reshape(t, D_MODEL)
    for i in range(DEPTH):
        j = i // 2
        if i % 2 == 0:
            h, q, k, v = _pre_mla_call(h, ffn_g, w_in, w_out, pos, mix_g, wd, q_g, wuq, kv_g,
                                       wukv, invf, i, j)
            w_o = mla_wo
        else:
            h, q, k, v = _pre_fox_call(h, ffn_g, w_in, w_out, mix_g, fox_w, fox_b, i, j,
                                       batch, seq)
            w_o = fox_wo
        attn = _attn_call(q.reshape(batch, seq, -1), k.reshape(batch, seq, -1),
                          v.reshape(batch, seq, -1), chunk_causal=(i % 2 == 0))
        h = _post_call(h, attn.reshape(t, -1), p_tok, w_o, ffn_g, w_in, w_out, ple_g, w_gate,
                       w_proj, fin_g, i, j, final=(i == DEPTH - 1))
    return h.reshape(batch, seq, D_MODEL)
```

```python
import functools

import jax
import jax.numpy as jnp
from jax import lax
from jax.experimental import pallas as pl
from jax.experimental.pallas import tpu as pltpu

D_MODEL = 1024
DEPTH = 4
CHUNK = 64
D_FF = 2816
PLE_DIM = 256
HEADS = 8
MLA_NOPE = 128
MLA_ROPE = 64
MLA_V = 128
MLA_Q_LORA = 256
MLA_KV_LORA = 128
ROPE_THETA = 10000.0
FOX_HEAD_DIM = 128
NORM_EPS = 1e-6
NEG_INF = -1e30

F32 = jnp.float32
BF16 = jnp.bfloat16

LANES = 128
QK_DIM = 2 * LANES
V_DIM = 128
TM = 512
FFN_COLS = 256
TQ = 256
HEADS_PER_STEP = 4
ROW_SPLIT = 2
SCORE_LOOKAHEAD = 3
BF16_SUBLANES = 16
VT_ROWS = V_DIM + BF16_SUBLANES
LOG2E = 1.4426950408889634
ROPE_PACK = LANES // (MLA_ROPE // 2)
VMEM_LIMIT = 56 * 1024 * 1024


def _rms(x, g):
    return x * lax.rsqrt(jnp.mean(x * x, axis=-1, keepdims=True) + NORM_EPS) * g


def _sigmoid(x):
    return 1.0 / (1.0 + jnp.exp(-x))


def _lane_roll(x, shift):
    return x if shift % LANES == 0 else pltpu.roll(x, shift % LANES, 1)


def _ffn_half_step(h, g, w_in_ref, w_out_ref, act_ref):
    xn = _rms(h, g).astype(BF16)
    for c in range(D_FF // FFN_COLS):
        lo = c * FFN_COLS
        gate = jnp.dot(xn, w_in_ref[:, lo:lo + FFN_COLS], preferred_element_type=F32)
        up = jnp.dot(xn, w_in_ref[:, D_FF + lo:D_FF + lo + FFN_COLS],
                     preferred_element_type=F32)
        act_ref[:, lo:lo + FFN_COLS] = (gate * _sigmoid(gate) * up).astype(BF16)
    y = jnp.dot(act_ref[...], w_out_ref[...], preferred_element_type=F32)
    return h + 0.5 * y


def _resident(block_shape, index):
    return pl.BlockSpec(block_shape, lambda *_: index, pipeline_mode=pl.Buffered(1))


def _params(n_axes):
    return pltpu.CompilerParams(
        dimension_semantics=("arbitrary",) * n_axes, vmem_limit_bytes=VMEM_LIMIT)


def _ffn_specs(layer, half):
    return [
        _resident((None, None, 1, D_MODEL), (layer, half, 0, 0)),
        _resident((D_MODEL, 2 * D_FF), (0, 0)),
        _resident((D_FF, D_MODEL), (0, 0)),
    ]


def _pre_mla_kernel(h_ref, fg_ref, w_in_ref, w_out_ref, pos_ref, g_ref, wd_ref, qn_ref, wuq_ref,
                    kvn_ref, wukv_ref, invf_ref, ho_ref, q_ref, k_ref, v_ref, act_ref):
    scale = (MLA_NOPE + MLA_ROPE) ** -0.5 * LOG2E
    h = _ffn_half_step(h_ref[...], fg_ref[...], w_in_ref, w_out_ref, act_ref)
    ho_ref[...] = h

    ang = pos_ref[...].astype(F32) * invf_ref[...]
    cosv = jnp.cos(ang)
    sinv = jnp.sin(ang)
    half = MLA_ROPE // 2
    lane = lax.broadcasted_iota(jnp.int32, (TM // ROPE_PACK, LANES), 1)
    cos_rows, sin_up_rows, sin_dn_rows = [], [], []
    for a in range(ROPE_PACK):
        lo_shift = (-a * half) % LANES
        hi_shift = (half - a * half) % LANES
        cos_lo, cos_hi = _lane_roll(cosv, lo_shift), _lane_roll(cosv, hi_shift)
        sin_lo, sin_hi = _lane_roll(sinv, lo_shift), _lane_roll(sinv, hi_shift)
        cos_rows.append(jnp.where(lane < half, cos_lo, jnp.where(lane < MLA_ROPE, cos_hi, 0.0)))
        sin_up_rows.append(jnp.where((lane >= half) & (lane < MLA_ROPE), sin_hi, 0.0))
        sin_dn_rows.append(jnp.where(lane < half, -sin_lo, 0.0))
    cos_t = jnp.concatenate(cos_rows, axis=0)
    sin_up = jnp.concatenate(sin_up_rows, axis=0)
    sin_dn = jnp.concatenate(sin_dn_rows, axis=0)

    rows = [slice(s * (TM // ROW_SPLIT), (s + 1) * (TM // ROW_SPLIT)) for s in range(ROW_SPLIT)]
    cs = [jnp.dot(_rms(h[r], g_ref[...]).astype(BF16), wd_ref[...],
                  preferred_element_type=F32) for r in rows]
    lins = []
    for c in cs:
        cq = _rms(c[:, :MLA_Q_LORA], qn_ref[...]).astype(BF16)
        ckv = _rms(c[:, MLA_Q_LORA:MLA_Q_LORA + MLA_KV_LORA], kvn_ref[...]).astype(BF16)
        lins.append((jnp.dot(cq, wuq_ref[...], preferred_element_type=F32),
                     jnp.dot(ckv, wukv_ref[...], preferred_element_type=F32)))
    for r, c, (q_lin, kv) in zip(rows, cs, lins):
        def rope(x):
            return (x * cos_t[r] + pltpu.roll(x, half, 1) * sin_up[r]
                    + pltpu.roll(x, LANES - half, 1) * sin_dn[r])

        k_rope = rope(c[:, MLA_Q_LORA + MLA_KV_LORA:]).astype(BF16)
        for hh in range(HEADS):
            q_nope = q_lin[:, hh * LANES:(hh + 1) * LANES]
            q_rot = q_lin[:, (HEADS + hh) * LANES:(HEADS + hh + 1) * LANES]
            q_ref[r, hh * QK_DIM:hh * QK_DIM + LANES] = (q_nope * scale).astype(BF16)
            q_ref[r, hh * QK_DIM + LANES:(hh + 1) * QK_DIM] = (rope(q_rot) * scale).astype(BF16)
            k_ref[r, hh * QK_DIM:hh * QK_DIM + LANES] = kv[:, hh * 256:hh * 256 + 128].astype(BF16)
            k_ref[r, hh * QK_DIM + LANES:(hh + 1) * QK_DIM] = k_rope
            v_ref[r, hh * V_DIM:(hh + 1) * V_DIM] = (
                kv[:, hh * 256 + 128:(hh + 1) * 256].astype(BF16))


def _pre_mla_call(h, ffn_norm, w_in, w_out, pos, mix_norm, wd, qn, wuq, kvn, wukv, invf, layer, j):
    t = h.shape[0]
    tok = lambda w: pl.BlockSpec((TM, w), lambda i: (i, 0))
    return pl.pallas_call(
        _pre_mla_kernel,
        grid=(t // TM,),
        in_specs=[
            tok(D_MODEL),
            *_ffn_specs(layer, 0),
            pl.BlockSpec((None, TM // ROPE_PACK, LANES), lambda i: (i, 0, 0)),
            _resident((None, 1, D_MODEL), (layer, 0, 0)),
            _resident((None, D_MODEL, 512), (j, 0, 0)),
            _resident((None, 1, MLA_Q_LORA), (j, 0, 0)),
            _resident((None, MLA_Q_LORA, 2 * HEADS * LANES), (j, 0, 0)),
            _resident((None, 1, MLA_KV_LORA), (j, 0, 0)),
            _resident((None, MLA_KV_LORA, HEADS * (MLA_NOPE + MLA_V)), (j, 0, 0)),
            _resident((1, LANES), (0, 0)),
        ],
        out_specs=[tok(D_MODEL), tok(HEADS * QK_DIM), tok(HEADS * QK_DIM), tok(HEADS * V_DIM)],
        out_shape=[
            jax.ShapeDtypeStruct((t, D_MODEL), F32),
            jax.ShapeDtypeStruct((t, HEADS * QK_DIM), BF16),
            jax.ShapeDtypeStruct((t, HEADS * QK_DIM), BF16),
            jax.ShapeDtypeStruct((t, HEADS * V_DIM), BF16),
        ],
        scratch_shapes=[pltpu.VMEM((TM, D_FF), BF16)],
        compiler_params=_params(1),
        name="pre_mla",
    )(h, ffn_norm, w_in, w_out, pos, mix_norm, wd, qn, wuq, kvn, wukv, invf)


def _split3(x):
    hi = x.astype(BF16).astype(F32)
    r = x - hi
    mid = r.astype(BF16).astype(F32)
    lo = r - mid
    return hi, mid, lo


def _pre_fox_kernel(h_ref, fg_ref, w_in_ref, w_out_ref, g_ref, w_ref, bf_ref,
                    ho_ref, q_ref, k_ref, v_ref, act_ref, carry_ref, tri_ref):
    hd = HEADS * FOX_HEAD_DIM
    scale = FOX_HEAD_DIM ** -0.5 * LOG2E

    @pl.when((pl.program_id(0) == 0) & (pl.program_id(1) == 0))
    def _():
        r = lax.broadcasted_iota(jnp.int32, (TM, TM), 0)
        cc = lax.broadcasted_iota(jnp.int32, (TM, TM), 1)
        tri_ref[...] = (cc <= r).astype(BF16)

    @pl.when(pl.program_id(1) == 0)
    def _():
        carry_ref[...] = jnp.zeros_like(carry_ref)

    h = _ffn_half_step(h_ref[...], fg_ref[...], w_in_ref, w_out_ref, act_ref)
    ho_ref[...] = h

    u = _rms(h, g_ref[...]).astype(BF16)
    f_logit = jnp.dot(u, w_ref[:, 3 * hd:], preferred_element_type=F32)
    q = jnp.dot(u, w_ref[:, :hd], preferred_element_type=F32)
    z = f_logit + bf_ref[...]
    log_f = -(jnp.maximum(-z, 0.0) + jnp.log1p(jnp.exp(-jnp.abs(z))))

    lane = lax.broadcasted_iota(jnp.int32, (TM, LANES), 1)
    hi, mid, lo = _split3(log_f)
    packed = jnp.where(lane < HEADS, hi,
                       jnp.where(lane < 2 * HEADS, _lane_roll(mid, HEADS),
                                 jnp.where(lane < 3 * HEADS, _lane_roll(lo, 2 * HEADS), 0.0)))
    psum = jnp.dot(tri_ref[...], packed.astype(BF16), preferred_element_type=F32)
    cum = ((_lane_roll(psum, -2 * HEADS) + _lane_roll(psum, -HEADS)) + psum) + carry_ref[...]
    carry_ref[...] = cum[TM - 1:TM, :]

    k = jnp.dot(u, w_ref[:, hd:2 * hd], preferred_element_type=F32)

    cum2 = cum * LOG2E
    parts = []
    for hh in range(HEADS):
        cb = jnp.broadcast_to(cum2[:, hh:hh + 1], (TM, LANES))
        c_hi, c_mid, c_lo = _split3(cb)
        parts.append(jnp.where((lane == 0) | (lane == 3), c_hi,
                               jnp.where((lane == 1) | (lane == 4), c_mid, c_lo)))
        q_ex = jnp.where(lane < 3, parts[hh], jnp.where(lane < 6, 1.0, 0.0))
        q_ref[:, hh * QK_DIM:hh * QK_DIM + LANES] = (
            q[:, hh * LANES:(hh + 1) * LANES] * scale).astype(BF16)
        q_ref[:, hh * QK_DIM + LANES:(hh + 1) * QK_DIM] = q_ex.astype(BF16)
    v = jnp.dot(u, w_ref[:, 2 * hd:3 * hd], preferred_element_type=F32)
    for hh in range(HEADS):
        k_ex = jnp.where(lane < 3, 1.0, jnp.where(lane < 6, -parts[hh], 0.0))
        k_ref[:, hh * QK_DIM:hh * QK_DIM + LANES] = k[:, hh * LANES:(hh + 1) * LANES].astype(BF16)
        k_ref[:, hh * QK_DIM + LANES:(hh + 1) * QK_DIM] = k_ex.astype(BF16)
    v_ref[...] = v.astype(BF16)


def _pre_fox_call(h, ffn_norm, w_in, w_out, mix_norm, fox_w, b_f, layer, j, batch, seq):
    t = h.shape[0]
    nt = seq // TM
    tok = lambda w: pl.BlockSpec((TM, w), lambda b, i: (b * nt + i, 0))
    return pl.pallas_call(
        _pre_fox_kernel,
        grid=(batch, nt),
        in_specs=[
            tok(D_MODEL),
            *_ffn_specs(layer, 0),
            _resident((None, 1, D_MODEL), (layer, 0, 0)),
            _resident((None, D_MODEL, 3 * HEADS * FOX_HEAD_DIM + LANES), (j, 0, 0)),
            _resident((None, 1, LANES), (j, 0, 0)),
        ],
        out_specs=[tok(D_MODEL), tok(HEADS * QK_DIM), tok(HEADS * QK_DIM), tok(HEADS * V_DIM)],
        out_shape=[
            jax.ShapeDtypeStruct((t, D_MODEL), F32),
            jax.ShapeDtypeStruct((t, HEADS * QK_DIM), BF16),
            jax.ShapeDtypeStruct((t, HEADS * QK_DIM), BF16),
            jax.ShapeDtypeStruct((t, HEADS * V_DIM), BF16),
        ],
        scratch_shapes=[pltpu.VMEM((TM, D_FF), BF16), pltpu.VMEM((1, LANES), F32),
                        pltpu.VMEM((TM, TM), BF16)],
        compiler_params=_params(2),
        name="pre_fox",
    )(h, ffn_norm, w_in, w_out, mix_norm, fox_w, b_f)


def _attn_kernel(q_ref, k_ref, v_ref, *rest, chunk_causal, seq, n_cast):
    o_ref, vt_ref = rest[n_cast], rest[-1]
    for src, dst in zip(rest[:n_cast], rest[n_cast + 1:-1]):
        dst[...] = src[...].astype(BF16)

    key = lax.broadcasted_iota(jnp.int32, (TQ, TQ), 0)
    qry = lax.broadcasted_iota(jnp.int32, (TQ, TQ), 1)
    if chunk_causal:
        shift = CHUNK.bit_length() - 1
        allowed = (key >> shift) <= (qry >> shift)
    else:
        allowed = key <= qry
    nt = (((1,), (1,)), ((), ()))

    for hh in range(HEADS_PER_STEP):
        vt_ref[hh, 0:V_DIM, :] = (
            v_ref[:, hh * V_DIM:(hh + 1) * V_DIM].astype(F32).T.astype(BF16))
        vt_ref[hh, V_DIM:, :] = jnp.ones((VT_ROWS - V_DIM, seq), BF16)

    def scores(hh, i):
        q0 = i * TQ
        qk = slice(hh * QK_DIM, (hh + 1) * QK_DIM)
        q = q_ref[q0:q0 + TQ, qk]
        s_dg = lax.dot_general(k_ref[q0:q0 + TQ, qk], q, nt, preferred_element_type=F32)
        s_dg = jnp.where(allowed, s_dg, NEG_INF)
        s_lo = None
        if i > 0:
            s_lo = lax.dot_general(k_ref[0:q0, qk], q, nt, preferred_element_type=F32)
        return s_dg, s_lo

    def finish(hh, i, s_dg, s_lo):
        q0 = i * TQ
        m = jnp.max(s_dg, axis=0, keepdims=True)
        if i > 0:
            m = jnp.maximum(m, jnp.max(s_lo, axis=0, keepdims=True))
        o_t = jnp.dot(vt_ref[hh, :, q0:q0 + TQ], jnp.exp2(s_dg - m).astype(BF16),
                      preferred_element_type=F32)
        if i > 0:
            o_t = o_t + jnp.dot(vt_ref[hh, :, 0:q0], jnp.exp2(s_lo - m).astype(BF16),
                                preferred_element_type=F32)
        out = o_t[0:V_DIM, :] / o_t[V_DIM:V_DIM + 1, :]
        o_ref[q0:q0 + TQ, hh * V_DIM:(hh + 1) * V_DIM] = out.T.astype(o_ref.dtype)

    items = [(hh, i) for hh in range(HEADS_PER_STEP) for i in range(seq // TQ)]
    pending = [scores(*it) for it in items[:SCORE_LOOKAHEAD]]
    for n, item in enumerate(items):
        if n + SCORE_LOOKAHEAD < len(items):
            pending.append(scores(*items[n + SCORE_LOOKAHEAD]))
        finish(*item, *pending.pop(0))


def _attn_call(q, k, v, chunk_causal, ffn_w_in, ffn_w_out, cast_slots):
    batch, seq, _ = q.shape
    hsteps = HEADS // HEADS_PER_STEP
    steps = batch * hsteps
    in_rows, out_rows = D_MODEL // steps, D_FF // steps
    assert in_rows * steps == D_MODEL and out_rows * steps == D_FF
    assert in_rows % BF16_SUBLANES == 0 and out_rows % BF16_SUBLANES == 0
    cast_in, cast_specs_in, cast_specs_out, cast_shapes = [], [], [], []
    for layer, half in cast_slots:
        for w, rows in ((ffn_w_in, in_rows), (ffn_w_out, out_rows)):
            cols = w.shape[-1]
            cast_in.append(w)
            cast_specs_in.append(pl.BlockSpec(
                (None, None, rows, cols),
                lambda b, h, layer=layer, half=half: (layer, half, b * hsteps + h, 0)))
            cast_specs_out.append(pl.BlockSpec((rows, cols), lambda b, h: (b * hsteps + h, 0)))
            cast_shapes.append(jax.ShapeDtypeStruct((rows * steps, cols), BF16))
    n_cast = len(cast_in)
    head_block = lambda w: pl.BlockSpec((None, seq, HEADS_PER_STEP * w), lambda b, h: (b, 0, h))
    outs = pl.pallas_call(
        functools.partial(_attn_kernel, chunk_causal=chunk_causal, seq=seq, n_cast=n_cast),
        grid=(batch, hsteps),
        in_specs=[head_block(QK_DIM), head_block(QK_DIM), head_block(V_DIM), *cast_specs_in],
        out_specs=[head_block(V_DIM), *cast_specs_out],
        out_shape=[jax.ShapeDtypeStruct((batch, seq, HEADS * V_DIM), BF16), *cast_shapes],
        scratch_shapes=[pltpu.VMEM((HEADS_PER_STEP, VT_ROWS, seq), BF16)],
        compiler_params=_params(2),
        name="attn",
    )(q, k, v, *cast_in)
    return outs[0], [tuple(outs[1 + 2 * s:3 + 2 * s]) for s in range(len(cast_slots))]


def _post_kernel(h_ref, a_ref, p_ref, wo_ref, g_ref, w_in_ref, w_out_ref, pg_ref, wg_ref,
                 wp_ref, fn_ref, o_ref, act_ref, *, final):
    h = h_ref[...] + jnp.dot(a_ref[...], wo_ref[...], preferred_element_type=F32)
    h = _ffn_half_step(h, g_ref[...], w_in_ref, w_out_ref, act_ref)
    gate = _sigmoid(jnp.dot(_rms(h, pg_ref[...]).astype(BF16), wg_ref[...],
                            preferred_element_type=F32))
    emb = jnp.dot(p_ref[...].astype(BF16), wp_ref[...], preferred_element_type=F32)
    h = h + gate * emb
    if final:
        h = _rms(h, fn_ref[...])
    o_ref[...] = h


def _post_call(h, attn, p, w_o, ffn_norm, w_in, w_out, ple_norm, w_gate, w_proj, final_norm,
               layer, j, final):
    t = h.shape[0]
    return pl.pallas_call(
        functools.partial(_post_kernel, final=final),
        grid=(t // TM,),
        in_specs=[
            pl.BlockSpec((TM, D_MODEL), lambda i: (i, 0)),
            pl.BlockSpec((TM, HEADS * V_DIM), lambda i: (i, 0)),
            pl.BlockSpec((None, TM, PLE_DIM), lambda i: (layer, i, 0)),
            _resident((None, HEADS * V_DIM, D_MODEL), (j, 0, 0)),
            *_ffn_specs(layer, 1),
            _resident((None, 1, D_MODEL), (layer, 0, 0)),
            _resident((None, D_MODEL, D_MODEL), (layer, 0, 0)),
            _resident((None, PLE_DIM, D_MODEL), (layer, 0, 0)),
            _resident((1, D_MODEL), (0, 0)),
        ],
        out_specs=pl.BlockSpec((TM, D_MODEL), lambda i: (i, 0)),
        out_shape=jax.ShapeDtypeStruct((t, D_MODEL), F32),
        scratch_shapes=[pltpu.VMEM((TM, D_FF), BF16)],
        compiler_params=_params(1),
        name="post",
    )(h, attn, p, w_o, ffn_norm, w_in, w_out, ple_norm, w_gate, w_proj, final_norm)


def kernel(x, p, positions, ffn_norm, ffn_w_in, ffn_w_out, mix_norm, mla_w_down, mla_q_norm,
           mla_w_uq, mla_kv_norm, mla_w_ukv, mla_w_o, fox_w_in, fox_b_f, fox_w_o, ple_norm,
           ple_w_gate, ple_w_proj, final_norm):
    batch, seq, _ = x.shape
    t = batch * seq
    n_mla = mla_w_down.shape[0]
    n_fox = fox_w_in.shape[0]

    pre_w = (ffn_w_in[0, 0].astype(BF16), ffn_w_out[0, 0].astype(BF16))
    ffn_g = ffn_norm.reshape(DEPTH, 2, 1, D_MODEL)
    mix_g = mix_norm.reshape(DEPTH, 1, D_MODEL)
    ple_g = ple_norm.reshape(DEPTH, 1, D_MODEL)
    fin_g = final_norm.reshape(1, D_MODEL)
    w_gate = ple_w_gate.astype(BF16)
    w_proj = ple_w_proj.astype(BF16)

    wd = jnp.pad(mla_w_down, ((0, 0), (0, 0), (0, LANES - MLA_ROPE))).astype(BF16)
    wq = mla_w_uq.reshape(n_mla, MLA_Q_LORA, HEADS, MLA_NOPE + MLA_ROPE)
    wq_nope = wq[..., :MLA_NOPE].reshape(n_mla, MLA_Q_LORA, HEADS * LANES)
    wq_rope = jnp.pad(wq[..., MLA_NOPE:], ((0, 0), (0, 0), (0, 0), (0, LANES - MLA_ROPE)))
    wuq = jnp.concatenate(
        [wq_nope, wq_rope.reshape(n_mla, MLA_Q_LORA, HEADS * LANES)], axis=-1).astype(BF16)
    wukv = mla_w_ukv.astype(BF16)
    q_g = mla_q_norm.reshape(n_mla, 1, MLA_Q_LORA)
    kv_g = mla_kv_norm.reshape(n_mla, 1, MLA_KV_LORA)
    mla_wo = mla_w_o.astype(BF16)
    inv_freq = ROPE_THETA ** (-jnp.arange(0, MLA_ROPE, 2, dtype=F32) / MLA_ROPE)
    invf = jnp.tile(inv_freq, ROPE_PACK)[None, :]

    fox_w = jnp.pad(fox_w_in, ((0, 0), (0, 0), (0, LANES - HEADS))).astype(BF16)
    fox_b = jnp.pad(fox_b_f, ((0, 0), (0, LANES - HEADS))).reshape(n_fox, 1, LANES)
    fox_wo = fox_w_o.astype(BF16)

    pos = positions.reshape(t // TM, ROPE_PACK, TM // ROPE_PACK).transpose(0, 2, 1)
    pos = jnp.repeat(pos, MLA_ROPE // 2, axis=2)
    p_tok = p.reshape(DEPTH, t, PLE_DIM)

    h = x.reshape(t, D_MODEL)
    for i in range(DEPTH):
        j = i // 2
        if i % 2 == 0:
            h, q, k, v = _pre_mla_call(h, ffn_g, *pre_w, pos, mix_g, wd, q_g, wuq, kv_g,
                                       wukv, invf, i, j)
            w_o = mla_wo
        else:
            h, q, k, v = _pre_fox_call(h, ffn_g, *pre_w, mix_g, fox_w, fox_b, i, j,
                                       batch, seq)
            w_o = fox_wo
        cast_slots = [(i, 1)] + ([(i + 1, 0)] if i + 1 < DEPTH else [])
        attn, cast = _attn_call(q.reshape(batch, seq, -1), k.reshape(batch, seq, -1),
                                v.reshape(batch, seq, -1), i % 2 == 0, ffn_w_in, ffn_w_out,
                                cast_slots)
        post_w, pre_w = cast[0], cast[-1]
        h = _post_call(h, attn.reshape(t, -1), p_tok, w_o, ffn_g, *post_w, ple_g, w_gate,
                       w_proj, fin_g, i, j, final=(i == DEPTH - 1))
    return h.reshape(batch, seq, D_MODEL)
```

```python
import functools

import jax
import jax.numpy as jnp
from jax import lax
from jax.experimental import pallas as pl
from jax.experimental.pallas import tpu as pltpu

D_MODEL = 1024
DEPTH = 4
CHUNK = 64
D_FF = 2816
PLE_DIM = 256
HEADS = 8
MLA_NOPE = 128
MLA_ROPE = 64
MLA_V = 128
MLA_Q_LORA = 256
MLA_KV_LORA = 128
ROPE_THETA = 10000.0
FOX_HEAD_DIM = 128
NORM_EPS = 1e-6
NEG_INF = -1e30

F32 = jnp.float32
BF16 = jnp.bfloat16

LANES = 128
QK_DIM = 2 * LANES
V_DIM = 128
TM = 512
FFN_COLS = 256
TQ = 256
HEADS_PER_STEP = 4
ROW_SPLIT = 2
SCORE_LOOKAHEAD = 3
BF16_SUBLANES = 16
VT_ROWS = V_DIM + BF16_SUBLANES
LOG2E = 1.4426950408889634
ROPE_PACK = LANES // (MLA_ROPE // 2)
VMEM_LIMIT = 56 * 1024 * 1024


def _rms(x, g):
    return x * lax.rsqrt(jnp.mean(x * x, axis=-1, keepdims=True) + NORM_EPS) * g


def _sigmoid(x):
    return 1.0 / (1.0 + jnp.exp(-x))


def _lane_roll(x, shift):
    return x if shift % LANES == 0 else pltpu.roll(x, shift % LANES, 1)


ROW_GROUPS = [slice(s * (TM // ROW_SPLIT), (s + 1) * (TM // ROW_SPLIT)) for s in range(ROW_SPLIT)]


def _ffn_half_step(h, g, w_in_ref, w_out_ref, act_ref):
    xn = _rms(h, g).astype(BF16)
    for c in range(D_FF // FFN_COLS):
        lo = c * FFN_COLS
        gate = jnp.dot(xn, w_in_ref[:, lo:lo + FFN_COLS], preferred_element_type=F32)
        up = jnp.dot(xn, w_in_ref[:, D_FF + lo:D_FF + lo + FFN_COLS],
                     preferred_element_type=F32)
        act_ref[:, lo:lo + FFN_COLS] = (gate * _sigmoid(gate) * up).astype(BF16)
    ys = [jnp.dot(act_ref[r, :], w_out_ref[...], preferred_element_type=F32) for r in ROW_GROUPS]
    return [h[r] + 0.5 * y for r, y in zip(ROW_GROUPS, ys)]


def _resident(block_shape, index):
    return pl.BlockSpec(block_shape, lambda *_: index, pipeline_mode=pl.Buffered(1))


def _params(n_axes):
    return pltpu.CompilerParams(
        dimension_semantics=("arbitrary",) * n_axes, vmem_limit_bytes=VMEM_LIMIT)


def _ffn_specs(layer, half):
    return [
        _resident((None, None, 1, D_MODEL), (layer, half, 0, 0)),
        _resident((D_MODEL, 2 * D_FF), (0, 0)),
        _resident((D_FF, D_MODEL), (0, 0)),
    ]


def _pre_mla_kernel(h_ref, fg_ref, w_in_ref, w_out_ref, pos_ref, g_ref, wd_ref, qn_ref, wuq_ref,
                    kvn_ref, wukv_ref, invf_ref, ho_ref, q_ref, k_ref, v_ref, act_ref):
    scale = (MLA_NOPE + MLA_ROPE) ** -0.5 * LOG2E
    hs = _ffn_half_step(h_ref[...], fg_ref[...], w_in_ref, w_out_ref, act_ref)
    for r, h in zip(ROW_GROUPS, hs):
        ho_ref[r, :] = h

    ang = pos_ref[...].astype(F32) * invf_ref[...]
    cosv = jnp.cos(ang)
    sinv = jnp.sin(ang)
    half = MLA_ROPE // 2
    lane = lax.broadcasted_iota(jnp.int32, (TM // ROPE_PACK, LANES), 1)
    cos_rows, sin_up_rows, sin_dn_rows = [], [], []
    for a in range(ROPE_PACK):
        lo_shift = (-a * half) % LANES
        hi_shift = (half - a * half) % LANES
        cos_lo, cos_hi = _lane_roll(cosv, lo_shift), _lane_roll(cosv, hi_shift)
        sin_lo, sin_hi = _lane_roll(sinv, lo_shift), _lane_roll(sinv, hi_shift)
        cos_rows.append(jnp.where(lane < half, cos_lo, jnp.where(lane < MLA_ROPE, cos_hi, 0.0)))
        sin_up_rows.append(jnp.where((lane >= half) & (lane < MLA_ROPE), sin_hi, 0.0))
        sin_dn_rows.append(jnp.where(lane < half, -sin_lo, 0.0))
    cos_t = jnp.concatenate(cos_rows, axis=0)
    sin_up = jnp.concatenate(sin_up_rows, axis=0)
    sin_dn = jnp.concatenate(sin_dn_rows, axis=0)

    rows = ROW_GROUPS
    cs = [jnp.dot(_rms(h, g_ref[...]).astype(BF16), wd_ref[...],
                  preferred_element_type=F32) for h in hs]
    lins = []
    for c in cs:
        cq = _rms(c[:, :MLA_Q_LORA], qn_ref[...]).astype(BF16)
        ckv = _rms(c[:, MLA_Q_LORA:MLA_Q_LORA + MLA_KV_LORA], kvn_ref[...]).astype(BF16)
        lins.append((jnp.dot(cq, wuq_ref[...], preferred_element_type=F32),
                     jnp.dot(ckv, wukv_ref[...], preferred_element_type=F32)))
    for r, c, (q_lin, kv) in zip(rows, cs, lins):
        def rope(x):
            return (x * cos_t[r] + pltpu.roll(x, half, 1) * sin_up[r]
                    + pltpu.roll(x, LANES - half, 1) * sin_dn[r])

        k_rope = rope(c[:, MLA_Q_LORA + MLA_KV_LORA:]).astype(BF16)
        for hh in range(HEADS):
            q_nope = q_lin[:, hh * LANES:(hh + 1) * LANES]
            q_rot = q_lin[:, (HEADS + hh) * LANES:(HEADS + hh + 1) * LANES]
            q_ref[r, hh * QK_DIM:hh * QK_DIM + LANES] = (q_nope * scale).astype(BF16)
            q_ref[r, hh * QK_DIM + LANES:(hh + 1) * QK_DIM] = (rope(q_rot) * scale).astype(BF16)
            k_ref[r, hh * QK_DIM:hh * QK_DIM + LANES] = kv[:, hh * 256:hh * 256 + 128].astype(BF16)
            k_ref[r, hh * QK_DIM + LANES:(hh + 1) * QK_DIM] = k_rope
            v_ref[r, hh * V_DIM:(hh + 1) * V_DIM] = (
                kv[:, hh * 256 + 128:(hh + 1) * 256].astype(BF16))


def _pre_mla_call(h, ffn_norm, w_in, w_out, pos, mix_norm, wd, qn, wuq, kvn, wukv, invf, layer, j):
    t = h.shape[0]
    tok = lambda w: pl.BlockSpec((TM, w), lambda i: (i, 0))
    return pl.pallas_call(
        _pre_mla_kernel,
        grid=(t // TM,),
        in_specs=[
            tok(D_MODEL),
            *_ffn_specs(layer, 0),
            pl.BlockSpec((None, TM // ROPE_PACK, LANES), lambda i: (i, 0, 0)),
            _resident((None, 1, D_MODEL), (layer, 0, 0)),
            _resident((None, D_MODEL, 512), (j, 0, 0)),
            _resident((None, 1, MLA_Q_LORA), (j, 0, 0)),
            _resident((None, MLA_Q_LORA, 2 * HEADS * LANES), (j, 0, 0)),
            _resident((None, 1, MLA_KV_LORA), (j, 0, 0)),
            _resident((None, MLA_KV_LORA, HEADS * (MLA_NOPE + MLA_V)), (j, 0, 0)),
            _resident((1, LANES), (0, 0)),
        ],
        out_specs=[tok(D_MODEL), tok(HEADS * QK_DIM), tok(HEADS * QK_DIM), tok(HEADS * V_DIM)],
        out_shape=[
            jax.ShapeDtypeStruct((t, D_MODEL), F32),
            jax.ShapeDtypeStruct((t, HEADS * QK_DIM), BF16),
            jax.ShapeDtypeStruct((t, HEADS * QK_DIM), BF16),
            jax.ShapeDtypeStruct((t, HEADS * V_DIM), BF16),
        ],
        scratch_shapes=[pltpu.VMEM((TM, D_FF), BF16)],
        compiler_params=_params(1),
        name="pre_mla",
    )(h, ffn_norm, w_in, w_out, pos, mix_norm, wd, qn, wuq, kvn, wukv, invf)


def _split3(x):
    hi = x.astype(BF16).astype(F32)
    r = x - hi
    mid = r.astype(BF16).astype(F32)
    lo = r - mid
    return hi, mid, lo


def _pre_fox_kernel(h_ref, fg_ref, w_in_ref, w_out_ref, g_ref, w_ref, bf_ref,
                    ho_ref, q_ref, k_ref, v_ref, act_ref, carry_ref, tri_ref):
    hd = HEADS * FOX_HEAD_DIM
    scale = FOX_HEAD_DIM ** -0.5 * LOG2E

    n_rows = TM // ROW_SPLIT

    @pl.when((pl.program_id(0) == 0) & (pl.program_id(1) == 0))
    def _():
        r = lax.broadcasted_iota(jnp.int32, (n_rows, n_rows), 0)
        cc = lax.broadcasted_iota(jnp.int32, (n_rows, n_rows), 1)
        tri_ref[...] = (cc <= r).astype(BF16)

    @pl.when(pl.program_id(1) == 0)
    def _():
        carry_ref[...] = jnp.zeros_like(carry_ref)

    hs = _ffn_half_step(h_ref[...], fg_ref[...], w_in_ref, w_out_ref, act_ref)
    lane = lax.broadcasted_iota(jnp.int32, (n_rows, LANES), 1)
    carry = carry_ref[...]
    for r, h in zip(ROW_GROUPS, hs):
        ho_ref[r, :] = h

        u = _rms(h, g_ref[...]).astype(BF16)
        f_logit = jnp.dot(u, w_ref[:, 3 * hd:], preferred_element_type=F32)
        q = jnp.dot(u, w_ref[:, :hd], preferred_element_type=F32)
        z = f_logit + bf_ref[...]
        log_f = -(jnp.maximum(-z, 0.0) + jnp.log1p(jnp.exp(-jnp.abs(z))))

        hi, mid, lo = _split3(log_f)
        packed = jnp.where(lane < HEADS, hi,
                           jnp.where(lane < 2 * HEADS, _lane_roll(mid, HEADS),
                                     jnp.where(lane < 3 * HEADS, _lane_roll(lo, 2 * HEADS), 0.0)))
        psum = jnp.dot(tri_ref[...], packed.astype(BF16), preferred_element_type=F32)
        cum = ((_lane_roll(psum, -2 * HEADS) + _lane_roll(psum, -HEADS)) + psum) + carry
        carry = cum[n_rows - 1:n_rows, :]

        k = jnp.dot(u, w_ref[:, hd:2 * hd], preferred_element_type=F32)

        cum2 = cum * LOG2E
        parts = []
        for hh in range(HEADS):
            cb = jnp.broadcast_to(cum2[:, hh:hh + 1], (n_rows, LANES))
            c_hi, c_mid, c_lo = _split3(cb)
            parts.append(jnp.where((lane == 0) | (lane == 3), c_hi,
                                   jnp.where((lane == 1) | (lane == 4), c_mid, c_lo)))
            q_ex = jnp.where(lane < 3, parts[hh], jnp.where(lane < 6, 1.0, 0.0))
            q_ref[r, hh * QK_DIM:hh * QK_DIM + LANES] = (
                q[:, hh * LANES:(hh + 1) * LANES] * scale).astype(BF16)
            q_ref[r, hh * QK_DIM + LANES:(hh + 1) * QK_DIM] = q_ex.astype(BF16)
        v = jnp.dot(u, w_ref[:, 2 * hd:3 * hd], preferred_element_type=F32)
        for hh in range(HEADS):
            k_ex = jnp.where(lane < 3, 1.0, jnp.where(lane < 6, -parts[hh], 0.0))
            k_ref[r, hh * QK_DIM:hh * QK_DIM + LANES] = (
                k[:, hh * LANES:(hh + 1) * LANES].astype(BF16))
            k_ref[r, hh * QK_DIM + LANES:(hh + 1) * QK_DIM] = k_ex.astype(BF16)
        v_ref[r, :] = v.astype(BF16)
    carry_ref[...] = carry


def _pre_fox_call(h, ffn_norm, w_in, w_out, mix_norm, fox_w, b_f, layer, j, batch, seq):
    t = h.shape[0]
    nt = seq // TM
    tok = lambda w: pl.BlockSpec((TM, w), lambda b, i: (b * nt + i, 0))
    return pl.pallas_call(
        _pre_fox_kernel,
        grid=(batch, nt),
        in_specs=[
            tok(D_MODEL),
            *_ffn_specs(layer, 0),
            _resident((None, 1, D_MODEL), (layer, 0, 0)),
            _resident((None, D_MODEL, 3 * HEADS * FOX_HEAD_DIM + LANES), (j, 0, 0)),
            _resident((None, 1, LANES), (j, 0, 0)),
        ],
        out_specs=[tok(D_MODEL), tok(HEADS * QK_DIM), tok(HEADS * QK_DIM), tok(HEADS * V_DIM)],
        out_shape=[
            jax.ShapeDtypeStruct((t, D_MODEL), F32),
            jax.ShapeDtypeStruct((t, HEADS * QK_DIM), BF16),
            jax.ShapeDtypeStruct((t, HEADS * QK_DIM), BF16),
            jax.ShapeDtypeStruct((t, HEADS * V_DIM), BF16),
        ],
        scratch_shapes=[pltpu.VMEM((TM, D_FF), BF16), pltpu.VMEM((1, LANES), F32),
                        pltpu.VMEM((TM // ROW_SPLIT, TM // ROW_SPLIT), BF16)],
        compiler_params=_params(2),
        name="pre_fox",
    )(h, ffn_norm, w_in, w_out, mix_norm, fox_w, b_f)


def _attn_kernel(q_ref, k_ref, v_ref, *rest, chunk_causal, seq, n_cast):
    o_ref, vt_ref = rest[n_cast], rest[-1]
    for src, dst in zip(rest[:n_cast], rest[n_cast + 1:-1]):
        dst[...] = src[...].astype(BF16)

    key = lax.broadcasted_iota(jnp.int32, (TQ, TQ), 0)
    qry = lax.broadcasted_iota(jnp.int32, (TQ, TQ), 1)
    if chunk_causal:
        shift = CHUNK.bit_length() - 1
        allowed = (key >> shift) <= (qry >> shift)
    else:
        allowed = key <= qry
    nt = (((1,), (1,)), ((), ()))

    for hh in range(HEADS_PER_STEP):
        vt_ref[hh, 0:V_DIM, :] = (
            v_ref[:, hh * V_DIM:(hh + 1) * V_DIM].astype(F32).T.astype(BF16))
        vt_ref[hh, V_DIM:, :] = jnp.ones((VT_ROWS - V_DIM, seq), BF16)

    def scores(hh, i):
        q0 = i * TQ
        qk = slice(hh * QK_DIM, (hh + 1) * QK_DIM)
        q = q_ref[q0:q0 + TQ, qk]
        s_dg = lax.dot_general(k_ref[q0:q0 + TQ, qk], q, nt, preferred_element_type=F32)
        s_dg = jnp.where(allowed, s_dg, NEG_INF)
        s_lo = None
        if i > 0:
            s_lo = lax.dot_general(k_ref[0:q0, qk], q, nt, preferred_element_type=F32)
        return s_dg, s_lo

    def finish(hh, i, s_dg, s_lo):
        q0 = i * TQ
        m = jnp.max(s_dg, axis=0, keepdims=True)
        if i > 0:
            m = jnp.maximum(m, jnp.max(s_lo, axis=0, keepdims=True))
        o_t = jnp.dot(vt_ref[hh, :, q0:q0 + TQ], jnp.exp2(s_dg - m).astype(BF16),
                      preferred_element_type=F32)
        if i > 0:
            o_t = o_t + jnp.dot(vt_ref[hh, :, 0:q0], jnp.exp2(s_lo - m).astype(BF16),
                                preferred_element_type=F32)
        out = o_t[0:V_DIM, :] / o_t[V_DIM:V_DIM + 1, :]
        o_ref[q0:q0 + TQ, hh * V_DIM:(hh + 1) * V_DIM] = out.T.astype(o_ref.dtype)

    items = [(hh, i) for hh in range(HEADS_PER_STEP) for i in range(seq // TQ)]
    pending = [scores(*it) for it in items[:SCORE_LOOKAHEAD]]
    for n, item in enumerate(items):
        if n + SCORE_LOOKAHEAD < len(items):
            pending.append(scores(*items[n + SCORE_LOOKAHEAD]))
        finish(*item, *pending.pop(0))


def _attn_call(q, k, v, chunk_causal, casts):
    batch, seq, _ = q.shape
    hsteps = HEADS // HEADS_PER_STEP
    steps = batch * hsteps
    cast_in, cast_specs_in, cast_specs_out, cast_shapes = [], [], [], []
    for w, lead in casts:
        n_rows, cols = w.shape[-2:]
        rows = n_rows // steps
        assert rows * steps == n_rows and rows % BF16_SUBLANES == 0
        cast_in.append(w)
        cast_specs_in.append(pl.BlockSpec(
            (None,) * len(lead) + (rows, cols),
            lambda b, h, lead=lead: (*lead, b * hsteps + h, 0)))
        cast_specs_out.append(pl.BlockSpec((rows, cols), lambda b, h: (b * hsteps + h, 0)))
        cast_shapes.append(jax.ShapeDtypeStruct((n_rows, cols), BF16))
    n_cast = len(cast_in)
    head_block = lambda w: pl.BlockSpec((None, seq, HEADS_PER_STEP * w), lambda b, h: (b, 0, h))
    outs = pl.pallas_call(
        functools.partial(_attn_kernel, chunk_causal=chunk_causal, seq=seq, n_cast=n_cast),
        grid=(batch, hsteps),
        in_specs=[head_block(QK_DIM), head_block(QK_DIM), head_block(V_DIM), *cast_specs_in],
        out_specs=[head_block(V_DIM), *cast_specs_out],
        out_shape=[jax.ShapeDtypeStruct((batch, seq, HEADS * V_DIM), BF16), *cast_shapes],
        scratch_shapes=[pltpu.VMEM((HEADS_PER_STEP, VT_ROWS, seq), BF16)],
        compiler_params=_params(2),
        name="attn",
    )(q, k, v, *cast_in)
    return outs[0], list(outs[1:])


def _post_kernel(h_ref, a_ref, p_ref, wo_ref, g_ref, w_in_ref, w_out_ref, pg_ref, wg_ref,
                 wp_ref, fn_ref, o_ref, act_ref, *, final):
    h = h_ref[...] + jnp.dot(a_ref[...], wo_ref[...], preferred_element_type=F32)
    hs = _ffn_half_step(h, g_ref[...], w_in_ref, w_out_ref, act_ref)
    for r, h in zip(ROW_GROUPS, hs):
        gate = _sigmoid(jnp.dot(_rms(h, pg_ref[...]).astype(BF16), wg_ref[...],
                                preferred_element_type=F32))
        emb = jnp.dot(p_ref[r, :].astype(BF16), wp_ref[...], preferred_element_type=F32)
        h = h + gate * emb
        if final:
            h = _rms(h, fn_ref[...])
        o_ref[r, :] = h


def _post_call(h, attn, p, w_o, ffn_norm, w_in, w_out, ple_norm, w_gate, w_proj, final_norm,
               layer, final):
    t = h.shape[0]
    return pl.pallas_call(
        functools.partial(_post_kernel, final=final),
        grid=(t // TM,),
        in_specs=[
            pl.BlockSpec((TM, D_MODEL), lambda i: (i, 0)),
            pl.BlockSpec((TM, HEADS * V_DIM), lambda i: (i, 0)),
            pl.BlockSpec((None, TM, PLE_DIM), lambda i: (layer, i, 0)),
            _resident((HEADS * V_DIM, D_MODEL), (0, 0)),
            *_ffn_specs(layer, 1),
            _resident((None, 1, D_MODEL), (layer, 0, 0)),
            _resident((D_MODEL, D_MODEL), (0, 0)),
            _resident((PLE_DIM, D_MODEL), (0, 0)),
            _resident((1, D_MODEL), (0, 0)),
        ],
        out_specs=pl.BlockSpec((TM, D_MODEL), lambda i: (i, 0)),
        out_shape=jax.ShapeDtypeStruct((t, D_MODEL), F32),
        scratch_shapes=[pltpu.VMEM((TM, D_FF), BF16)],
        compiler_params=_params(1),
        name="post",
    )(h, attn, p, w_o, ffn_norm, w_in, w_out, ple_norm, w_gate, w_proj, final_norm)


def kernel(x, p, positions, ffn_norm, ffn_w_in, ffn_w_out, mix_norm, mla_w_down, mla_q_norm,
           mla_w_uq, mla_kv_norm, mla_w_ukv, mla_w_o, fox_w_in, fox_b_f, fox_w_o, ple_norm,
           ple_w_gate, ple_w_proj, final_norm):
    batch, seq, _ = x.shape
    t = batch * seq
    n_mla = mla_w_down.shape[0]
    n_fox = fox_w_in.shape[0]

    pre_w = (ffn_w_in[0, 0].astype(BF16), ffn_w_out[0, 0].astype(BF16))
    ffn_g = ffn_norm.reshape(DEPTH, 2, 1, D_MODEL)
    mix_g = mix_norm.reshape(DEPTH, 1, D_MODEL)
    ple_g = ple_norm.reshape(DEPTH, 1, D_MODEL)
    fin_g = final_norm.reshape(1, D_MODEL)

    wd = jnp.pad(mla_w_down, ((0, 0), (0, 0), (0, LANES - MLA_ROPE))).astype(BF16)
    wq = mla_w_uq.reshape(n_mla, MLA_Q_LORA, HEADS, MLA_NOPE + MLA_ROPE)
    wq_nope = wq[..., :MLA_NOPE].reshape(n_mla, MLA_Q_LORA, HEADS * LANES)
    wq_rope = jnp.pad(wq[..., MLA_NOPE:], ((0, 0), (0, 0), (0, 0), (0, LANES - MLA_ROPE)))
    wuq = jnp.concatenate(
        [wq_nope, wq_rope.reshape(n_mla, MLA_Q_LORA, HEADS * LANES)], axis=-1).astype(BF16)
    wukv = mla_w_ukv.astype(BF16)
    q_g = mla_q_norm.reshape(n_mla, 1, MLA_Q_LORA)
    kv_g = mla_kv_norm.reshape(n_mla, 1, MLA_KV_LORA)
    inv_freq = ROPE_THETA ** (-jnp.arange(0, MLA_ROPE, 2, dtype=F32) / MLA_ROPE)
    invf = jnp.tile(inv_freq, ROPE_PACK)[None, :]

    fox_w = jnp.pad(fox_w_in, ((0, 0), (0, 0), (0, LANES - HEADS))).astype(BF16)
    fox_b = jnp.pad(fox_b_f, ((0, 0), (0, LANES - HEADS))).reshape(n_fox, 1, LANES)

    pos = positions.reshape(t // TM, ROPE_PACK, TM // ROPE_PACK).transpose(0, 2, 1)
    pos = jnp.repeat(pos, MLA_ROPE // 2, axis=2)
    p_tok = p.reshape(DEPTH, t, PLE_DIM)

    h = x.reshape(t, D_MODEL)
    for i in range(DEPTH):
        j = i // 2
        if i % 2 == 0:
            h, q, k, v = _pre_mla_call(h, ffn_g, *pre_w, pos, mix_g, wd, q_g, wuq, kv_g,
                                       wukv, invf, i, j)
            w_o = mla_w_o
        else:
            h, q, k, v = _pre_fox_call(h, ffn_g, *pre_w, mix_g, fox_w, fox_b, i, j,
                                       batch, seq)
            w_o = fox_w_o
        casts = [(w_o, (j,)), (ffn_w_in, (i, 1)), (ffn_w_out, (i, 1)), (ple_w_gate, (i,)),
                 (ple_w_proj, (i,))]
        if i + 1 < DEPTH:
            casts += [(ffn_w_in, (i + 1, 0)), (ffn_w_out, (i + 1, 0))]
        attn, cast = _attn_call(q.reshape(batch, seq, -1), k.reshape(batch, seq, -1),
                                v.reshape(batch, seq, -1), i % 2 == 0, casts)
        wo_b, w_in_b, w_out_b, w_gate_b, w_proj_b = cast[:5]
        pre_w = tuple(cast[5:])
        h = _post_call(h, attn.reshape(t, -1), p_tok, wo_b, ffn_g, w_in_b, w_out_b, ple_g,
                       w_gate_b, w_proj_b, fin_g, i, final=(i == DEPTH - 1))
    return h.reshape(batch, seq, D_MODEL)
```

```python
import functools

import jax
import jax.numpy as jnp
from jax import lax
from jax.experimental import pallas as pl
from jax.experimental.pallas import tpu as pltpu

D_MODEL = 1024
DEPTH = 4
CHUNK = 64
D_FF = 2816
PLE_DIM = 256
HEADS = 8
MLA_NOPE = 128
MLA_ROPE = 64
MLA_V = 128
MLA_Q_LORA = 256
MLA_KV_LORA = 128
ROPE_THETA = 10000.0
FOX_HEAD_DIM = 128
NORM_EPS = 1e-6
NEG_INF = -1e30

F32 = jnp.float32
BF16 = jnp.bfloat16

LANES = 128
QK_DIM = 2 * LANES
V_DIM = 128
TM = 512
POST_TM = 1024
FFN_COLS = 256
TQ = 256
HEADS_PER_STEP = 4
ROW_SPLIT = 2
SCORE_LOOKAHEAD = 3
BF16_SUBLANES = 16
VT_ROWS = V_DIM + BF16_SUBLANES
LOG2E = 1.4426950408889634
ROPE_PACK = LANES // (MLA_ROPE // 2)
VMEM_LIMIT = 58 * 1024 * 1024


def _rms(x, g):
    return x * lax.rsqrt(jnp.mean(x * x, axis=-1, keepdims=True) + NORM_EPS) * g


def _sigmoid(x):
    return 1.0 / (1.0 + jnp.exp(-x))


def _lane_roll(x, shift):
    return x if shift % LANES == 0 else pltpu.roll(x, shift % LANES, 1)


def _row_groups(n_rows):
    return [slice(s * (n_rows // ROW_SPLIT), (s + 1) * (n_rows // ROW_SPLIT))
            for s in range(ROW_SPLIT)]


ROW_GROUPS = _row_groups(TM)


def _ffn_half_step(h, g, w_in_ref, w_out_ref, act_ref):
    xn = _rms(h, g).astype(BF16)
    for c in range(D_FF // FFN_COLS):
        lo = c * FFN_COLS
        gate = jnp.dot(xn, w_in_ref[:, lo:lo + FFN_COLS], preferred_element_type=F32)
        up = jnp.dot(xn, w_in_ref[:, D_FF + lo:D_FF + lo + FFN_COLS],
                     preferred_element_type=F32)
        act_ref[:, lo:lo + FFN_COLS] = (gate * _sigmoid(gate) * up).astype(BF16)
    groups = _row_groups(h.shape[0])
    ys = [jnp.dot(act_ref[r, :], w_out_ref[...], preferred_element_type=F32) for r in groups]
    return [h[r] + 0.5 * y for r, y in zip(groups, ys)]


def _resident(block_shape, index):
    return pl.BlockSpec(block_shape, lambda *_: index, pipeline_mode=pl.Buffered(1))


def _params(n_axes):
    return pltpu.CompilerParams(
        dimension_semantics=("arbitrary",) * n_axes, vmem_limit_bytes=VMEM_LIMIT)


def _ffn_specs(layer, half):
    return [
        _resident((None, None, 1, D_MODEL), (layer, half, 0, 0)),
        _resident((D_MODEL, 2 * D_FF), (0, 0)),
        _resident((D_FF, D_MODEL), (0, 0)),
    ]


def _pre_mla_kernel(h_ref, fg_ref, w_in_ref, w_out_ref, pos_ref, g_ref, wd_ref, qn_ref, wuq_ref,
                    kvn_ref, wukv_ref, invf_ref, ho_ref, q_ref, k_ref, v_ref, act_ref):
    scale = (MLA_NOPE + MLA_ROPE) ** -0.5 * LOG2E
    hs = _ffn_half_step(h_ref[...], fg_ref[...], w_in_ref, w_out_ref, act_ref)
    for r, h in zip(ROW_GROUPS, hs):
        ho_ref[r, :] = h

    ang = pos_ref[...].astype(F32) * invf_ref[...]
    cosv = jnp.cos(ang)
    sinv = jnp.sin(ang)
    half = MLA_ROPE // 2
    lane = lax.broadcasted_iota(jnp.int32, (TM // ROPE_PACK, LANES), 1)
    cos_rows, sin_up_rows, sin_dn_rows = [], [], []
    for a in range(ROPE_PACK):
        lo_shift = (-a * half) % LANES
        hi_shift = (half - a * half) % LANES
        cos_lo, cos_hi = _lane_roll(cosv, lo_shift), _lane_roll(cosv, hi_shift)
        sin_lo, sin_hi = _lane_roll(sinv, lo_shift), _lane_roll(sinv, hi_shift)
        cos_rows.append(jnp.where(lane < half, cos_lo, jnp.where(lane < MLA_ROPE, cos_hi, 0.0)))
        sin_up_rows.append(jnp.where((lane >= half) & (lane < MLA_ROPE), sin_hi, 0.0))
        sin_dn_rows.append(jnp.where(lane < half, -sin_lo, 0.0))
    cos_t = jnp.concatenate(cos_rows, axis=0)
    sin_up = jnp.concatenate(sin_up_rows, axis=0)
    sin_dn = jnp.concatenate(sin_dn_rows, axis=0)

    rows = ROW_GROUPS
    cs = [jnp.dot(_rms(h, g_ref[...]).astype(BF16), wd_ref[...],
                  preferred_element_type=F32) for h in hs]
    lins = []
    for c in cs:
        cq = _rms(c[:, :MLA_Q_LORA], qn_ref[...]).astype(BF16)
        ckv = _rms(c[:, MLA_Q_LORA:MLA_Q_LORA + MLA_KV_LORA], kvn_ref[...]).astype(BF16)
        lins.append((jnp.dot(cq, wuq_ref[...], preferred_element_type=F32),
                     jnp.dot(ckv, wukv_ref[...], preferred_element_type=F32)))
    for r, c, (q_lin, kv) in zip(rows, cs, lins):
        def rope(x):
            return (x * cos_t[r] + pltpu.roll(x, half, 1) * sin_up[r]
                    + pltpu.roll(x, LANES - half, 1) * sin_dn[r])

        k_rope = rope(c[:, MLA_Q_LORA + MLA_KV_LORA:]).astype(BF16)
        for hh in range(HEADS):
            q_nope = q_lin[:, hh * LANES:(hh + 1) * LANES]
            q_rot = q_lin[:, (HEADS + hh) * LANES:(HEADS + hh + 1) * LANES]
            q_ref[r, hh * QK_DIM:hh * QK_DIM + LANES] = (q_nope * scale).astype(BF16)
            q_ref[r, hh * QK_DIM + LANES:(hh + 1) * QK_DIM] = (rope(q_rot) * scale).astype(BF16)
            k_ref[r, hh * QK_DIM:hh * QK_DIM + LANES] = kv[:, hh * 256:hh * 256 + 128].astype(BF16)
            k_ref[r, hh * QK_DIM + LANES:(hh + 1) * QK_DIM] = k_rope
            v_ref[r, hh * V_DIM:(hh + 1) * V_DIM] = (
                kv[:, hh * 256 + 128:(hh + 1) * 256].astype(BF16))


def _pre_mla_call(h, ffn_norm, w_in, w_out, pos, mix_norm, wd, qn, wuq, kvn, wukv, invf, layer, j):
    t = h.shape[0]
    tok = lambda w: pl.BlockSpec((TM, w), lambda i: (i, 0))
    return pl.pallas_call(
        _pre_mla_kernel,
        grid=(t // TM,),
        in_specs=[
            tok(D_MODEL),
            *_ffn_specs(layer, 0),
            pl.BlockSpec((None, TM // ROPE_PACK, LANES), lambda i: (i, 0, 0)),
            _resident((None, 1, D_MODEL), (layer, 0, 0)),
            _resident((None, D_MODEL, 512), (j, 0, 0)),
            _resident((None, 1, MLA_Q_LORA), (j, 0, 0)),
            _resident((None, MLA_Q_LORA, 2 * HEADS * LANES), (j, 0, 0)),
            _resident((None, 1, MLA_KV_LORA), (j, 0, 0)),
            _resident((None, MLA_KV_LORA, HEADS * (MLA_NOPE + MLA_V)), (j, 0, 0)),
            _resident((1, LANES), (0, 0)),
        ],
        out_specs=[tok(D_MODEL), tok(HEADS * QK_DIM), tok(HEADS * QK_DIM), tok(HEADS * V_DIM)],
        out_shape=[
            jax.ShapeDtypeStruct((t, D_MODEL), F32),
            jax.ShapeDtypeStruct((t, HEADS * QK_DIM), BF16),
            jax.ShapeDtypeStruct((t, HEADS * QK_DIM), BF16),
            jax.ShapeDtypeStruct((t, HEADS * V_DIM), BF16),
        ],
        scratch_shapes=[pltpu.VMEM((TM, D_FF), BF16)],
        compiler_params=_params(1),
        name="pre_mla",
    )(h, ffn_norm, w_in, w_out, pos, mix_norm, wd, qn, wuq, kvn, wukv, invf)


def _split3(x):
    hi = x.astype(BF16).astype(F32)
    r = x - hi
    mid = r.astype(BF16).astype(F32)
    lo = r - mid
    return hi, mid, lo


def _pre_fox_kernel(h_ref, fg_ref, w_in_ref, w_out_ref, g_ref, w_ref, bf_ref,
                    ho_ref, q_ref, k_ref, v_ref, act_ref, carry_ref, tri_ref):
    hd = HEADS * FOX_HEAD_DIM
    scale = FOX_HEAD_DIM ** -0.5 * LOG2E

    n_rows = TM // ROW_SPLIT

    @pl.when((pl.program_id(0) == 0) & (pl.program_id(1) == 0))
    def _():
        r = lax.broadcasted_iota(jnp.int32, (n_rows, n_rows), 0)
        cc = lax.broadcasted_iota(jnp.int32, (n_rows, n_rows), 1)
        tri_ref[...] = (cc <= r).astype(BF16)

    @pl.when(pl.program_id(1) == 0)
    def _():
        carry_ref[...] = jnp.zeros_like(carry_ref)

    hs = _ffn_half_step(h_ref[...], fg_ref[...], w_in_ref, w_out_ref, act_ref)
    lane = lax.broadcasted_iota(jnp.int32, (n_rows, LANES), 1)
    carry = carry_ref[...]
    for r, h in zip(ROW_GROUPS, hs):
        ho_ref[r, :] = h

        u = _rms(h, g_ref[...]).astype(BF16)
        f_logit = jnp.dot(u, w_ref[:, 3 * hd:], preferred_element_type=F32)
        q = jnp.dot(u, w_ref[:, :hd], preferred_element_type=F32)
        z = f_logit + bf_ref[...]
        log_f = -(jnp.maximum(-z, 0.0) + jnp.log1p(jnp.exp(-jnp.abs(z))))

        hi, mid, lo = _split3(log_f)
        packed = jnp.where(lane < HEADS, hi,
                           jnp.where(lane < 2 * HEADS, _lane_roll(mid, HEADS),
                                     jnp.where(lane < 3 * HEADS, _lane_roll(lo, 2 * HEADS), 0.0)))
        psum = jnp.dot(tri_ref[...], packed.astype(BF16), preferred_element_type=F32)
        cum = ((_lane_roll(psum, -2 * HEADS) + _lane_roll(psum, -HEADS)) + psum) + carry
        carry = cum[n_rows - 1:n_rows, :]

        k = jnp.dot(u, w_ref[:, hd:2 * hd], preferred_element_type=F32)

        cum2 = cum * LOG2E
        parts = []
        for hh in range(HEADS):
            cb = jnp.broadcast_to(cum2[:, hh:hh + 1], (n_rows, LANES))
            c_hi, c_mid, c_lo = _split3(cb)
            parts.append(jnp.where((lane == 0) | (lane == 3), c_hi,
                                   jnp.where((lane == 1) | (lane == 4), c_mid, c_lo)))
            q_ex = jnp.where(lane < 3, parts[hh], jnp.where(lane < 6, 1.0, 0.0))
            q_ref[r, hh * QK_DIM:hh * QK_DIM + LANES] = (
                q[:, hh * LANES:(hh + 1) * LANES] * scale).astype(BF16)
            q_ref[r, hh * QK_DIM + LANES:(hh + 1) * QK_DIM] = q_ex.astype(BF16)
        v = jnp.dot(u, w_ref[:, 2 * hd:3 * hd], preferred_element_type=F32)
        for hh in range(HEADS):
            k_ex = jnp.where(lane < 3, 1.0, jnp.where(lane < 6, -parts[hh], 0.0))
            k_ref[r, hh * QK_DIM:hh * QK_DIM + LANES] = (
                k[:, hh * LANES:(hh + 1) * LANES].astype(BF16))
            k_ref[r, hh * QK_DIM + LANES:(hh + 1) * QK_DIM] = k_ex.astype(BF16)
        v_ref[r, :] = v.astype(BF16)
    carry_ref[...] = carry


def _pre_fox_call(h, ffn_norm, w_in, w_out, mix_norm, fox_w, b_f, layer, j, batch, seq):
    t = h.shape[0]
    nt = seq // TM
    tok = lambda w: pl.BlockSpec((TM, w), lambda b, i: (b * nt + i, 0))
    return pl.pallas_call(
        _pre_fox_kernel,
        grid=(batch, nt),
        in_specs=[
            tok(D_MODEL),
            *_ffn_specs(layer, 0),
            _resident((None, 1, D_MODEL), (layer, 0, 0)),
            _resident((None, D_MODEL, 3 * HEADS * FOX_HEAD_DIM + LANES), (j, 0, 0)),
            _resident((None, 1, LANES), (j, 0, 0)),
        ],
        out_specs=[tok(D_MODEL), tok(HEADS * QK_DIM), tok(HEADS * QK_DIM), tok(HEADS * V_DIM)],
        out_shape=[
            jax.ShapeDtypeStruct((t, D_MODEL), F32),
            jax.ShapeDtypeStruct((t, HEADS * QK_DIM), BF16),
            jax.ShapeDtypeStruct((t, HEADS * QK_DIM), BF16),
            jax.ShapeDtypeStruct((t, HEADS * V_DIM), BF16),
        ],
        scratch_shapes=[pltpu.VMEM((TM, D_FF), BF16), pltpu.VMEM((1, LANES), F32),
                        pltpu.VMEM((TM // ROW_SPLIT, TM // ROW_SPLIT), BF16)],
        compiler_params=_params(2),
        name="pre_fox",
    )(h, ffn_norm, w_in, w_out, mix_norm, fox_w, b_f)


def _attn_kernel(q_ref, k_ref, v_ref, *rest, chunk_causal, seq, n_cast):
    o_ref, vt_ref = rest[n_cast], rest[-1]
    for src, dst in zip(rest[:n_cast], rest[n_cast + 1:-1]):
        dst[...] = src[...].astype(BF16)

    key = lax.broadcasted_iota(jnp.int32, (TQ, TQ), 0)
    qry = lax.broadcasted_iota(jnp.int32, (TQ, TQ), 1)
    if chunk_causal:
        shift = CHUNK.bit_length() - 1
        allowed = (key >> shift) <= (qry >> shift)
    else:
        allowed = key <= qry
    nt = (((1,), (1,)), ((), ()))

    for hh in range(HEADS_PER_STEP):
        vt_ref[hh, 0:V_DIM, :] = (
            v_ref[:, hh * V_DIM:(hh + 1) * V_DIM].astype(F32).T.astype(BF16))
        vt_ref[hh, V_DIM:, :] = jnp.ones((VT_ROWS - V_DIM, seq), BF16)

    def scores(hh, i):
        q0 = i * TQ
        qk = slice(hh * QK_DIM, (hh + 1) * QK_DIM)
        q = q_ref[q0:q0 + TQ, qk]
        s_dg = lax.dot_general(k_ref[q0:q0 + TQ, qk], q, nt, preferred_element_type=F32)
        s_dg = jnp.where(allowed, s_dg, NEG_INF)
        s_lo = None
        if i > 0:
            s_lo = lax.dot_general(k_ref[0:q0, qk], q, nt, preferred_element_type=F32)
        return s_dg, s_lo

    def finish(hh, i, s_dg, s_lo):
        q0 = i * TQ
        m = jnp.max(s_dg, axis=0, keepdims=True)
        if i > 0:
            m = jnp.maximum(m, jnp.max(s_lo, axis=0, keepdims=True))
        o_t = jnp.dot(vt_ref[hh, :, q0:q0 + TQ], jnp.exp2(s_dg - m).astype(BF16),
                      preferred_element_type=F32)
        if i > 0:
            o_t = o_t + jnp.dot(vt_ref[hh, :, 0:q0], jnp.exp2(s_lo - m).astype(BF16),
                                preferred_element_type=F32)
        out = o_t[0:V_DIM, :] / o_t[V_DIM:V_DIM + 1, :]
        o_ref[q0:q0 + TQ, hh * V_DIM:(hh + 1) * V_DIM] = out.T.astype(o_ref.dtype)

    items = [(hh, i) for hh in range(HEADS_PER_STEP) for i in range(seq // TQ)]
    pending = [scores(*it) for it in items[:SCORE_LOOKAHEAD]]
    for n, item in enumerate(items):
        if n + SCORE_LOOKAHEAD < len(items):
            pending.append(scores(*items[n + SCORE_LOOKAHEAD]))
        finish(*item, *pending.pop(0))


def _attn_call(q, k, v, chunk_causal, casts):
    batch, seq, _ = q.shape
    hsteps = HEADS // HEADS_PER_STEP
    steps = batch * hsteps
    cast_in, cast_specs_in, cast_specs_out, cast_shapes = [], [], [], []
    for w, lead in casts:
        n_rows, cols = w.shape[-2:]
        rows = n_rows // steps
        assert rows * steps == n_rows and rows % BF16_SUBLANES == 0
        cast_in.append(w)
        cast_specs_in.append(pl.BlockSpec(
            (None,) * len(lead) + (rows, cols),
            lambda b, h, lead=lead: (*lead, b * hsteps + h, 0)))
        cast_specs_out.append(pl.BlockSpec((rows, cols), lambda b, h: (b * hsteps + h, 0)))
        cast_shapes.append(jax.ShapeDtypeStruct((n_rows, cols), BF16))
    n_cast = len(cast_in)
    head_block = lambda w: pl.BlockSpec((None, seq, HEADS_PER_STEP * w), lambda b, h: (b, 0, h))
    outs = pl.pallas_call(
        functools.partial(_attn_kernel, chunk_causal=chunk_causal, seq=seq, n_cast=n_cast),
        grid=(batch, hsteps),
        in_specs=[head_block(QK_DIM), head_block(QK_DIM), head_block(V_DIM), *cast_specs_in],
        out_specs=[head_block(V_DIM), *cast_specs_out],
        out_shape=[jax.ShapeDtypeStruct((batch, seq, HEADS * V_DIM), BF16), *cast_shapes],
        scratch_shapes=[pltpu.VMEM((HEADS_PER_STEP, VT_ROWS, seq), BF16)],
        compiler_params=_params(2),
        name="attn",
    )(q, k, v, *cast_in)
    return outs[0], list(outs[1:])


def _post_kernel(h_ref, a_ref, p_ref, wo_ref, g_ref, w_in_ref, w_out_ref, pg_ref, wg_ref,
                 wp_ref, fn_ref, o_ref, act_ref, *, final):
    h = h_ref[...] + jnp.dot(a_ref[...], wo_ref[...], preferred_element_type=F32)
    hs = _ffn_half_step(h, g_ref[...], w_in_ref, w_out_ref, act_ref)
    for r, h in zip(_row_groups(POST_TM), hs):
        gate = _sigmoid(jnp.dot(_rms(h, pg_ref[...]).astype(BF16), wg_ref[...],
                                preferred_element_type=F32))
        emb = jnp.dot(p_ref[r, :].astype(BF16), wp_ref[...], preferred_element_type=F32)
        h = h + gate * emb
        if final:
            h = _rms(h, fn_ref[...])
        o_ref[r, :] = h


def _post_call(h, attn, p, w_o, ffn_norm, w_in, w_out, ple_norm, w_gate, w_proj, final_norm,
               layer, final):
    t = h.shape[0]
    return pl.pallas_call(
        functools.partial(_post_kernel, final=final),
        grid=(t // POST_TM,),
        in_specs=[
            pl.BlockSpec((POST_TM, D_MODEL), lambda i: (i, 0)),
            pl.BlockSpec((POST_TM, HEADS * V_DIM), lambda i: (i, 0)),
            pl.BlockSpec((None, POST_TM, PLE_DIM), lambda i: (layer, i, 0)),
            _resident((HEADS * V_DIM, D_MODEL), (0, 0)),
            *_ffn_specs(layer, 1),
            _resident((None, 1, D_MODEL), (layer, 0, 0)),
            _resident((D_MODEL, D_MODEL), (0, 0)),
            _resident((PLE_DIM, D_MODEL), (0, 0)),
            _resident((1, D_MODEL), (0, 0)),
        ],
        out_specs=pl.BlockSpec((POST_TM, D_MODEL), lambda i: (i, 0)),
        out_shape=jax.ShapeDtypeStruct((t, D_MODEL), F32),
        scratch_shapes=[pltpu.VMEM((POST_TM, D_FF), BF16)],
        compiler_params=_params(1),
        name="post",
    )(h, attn, p, w_o, ffn_norm, w_in, w_out, ple_norm, w_gate, w_proj, final_norm)


def kernel(x, p, positions, ffn_norm, ffn_w_in, ffn_w_out, mix_norm, mla_w_down, mla_q_norm,
           mla_w_uq, mla_kv_norm, mla_w_ukv, mla_w_o, fox_w_in, fox_b_f, fox_w_o, ple_norm,
           ple_w_gate, ple_w_proj, final_norm):
    batch, seq, _ = x.shape
    t = batch * seq
    n_mla = mla_w_down.shape[0]
    n_fox = fox_w_in.shape[0]

    pre_w = (ffn_w_in[0, 0].astype(BF16), ffn_w_out[0, 0].astype(BF16))
    ffn_g = ffn_norm.reshape(DEPTH, 2, 1, D_MODEL)
    mix_g = mix_norm.reshape(DEPTH, 1, D_MODEL)
    ple_g = ple_norm.reshape(DEPTH, 1, D_MODEL)
    fin_g = final_norm.reshape(1, D_MODEL)

    wd = jnp.pad(mla_w_down, ((0, 0), (0, 0), (0, LANES - MLA_ROPE))).astype(BF16)
    wq = mla_w_uq.reshape(n_mla, MLA_Q_LORA, HEADS, MLA_NOPE + MLA_ROPE)
    wq_nope = wq[..., :MLA_NOPE].reshape(n_mla, MLA_Q_LORA, HEADS * LANES)
    wq_rope = jnp.pad(wq[..., MLA_NOPE:], ((0, 0), (0, 0), (0, 0), (0, LANES - MLA_ROPE)))
    wuq = jnp.concatenate(
        [wq_nope, wq_rope.reshape(n_mla, MLA_Q_LORA, HEADS * LANES)], axis=-1).astype(BF16)
    wukv = mla_w_ukv.astype(BF16)
    q_g = mla_q_norm.reshape(n_mla, 1, MLA_Q_LORA)
    kv_g = mla_kv_norm.reshape(n_mla, 1, MLA_KV_LORA)
    inv_freq = ROPE_THETA ** (-jnp.arange(0, MLA_ROPE, 2, dtype=F32) / MLA_ROPE)
    invf = jnp.tile(inv_freq, ROPE_PACK)[None, :]

    fox_w = jnp.pad(fox_w_in, ((0, 0), (0, 0), (0, LANES - HEADS))).astype(BF16)
    fox_b = jnp.pad(fox_b_f, ((0, 0), (0, LANES - HEADS))).reshape(n_fox, 1, LANES)

    pos = positions.reshape(t // TM, ROPE_PACK, TM // ROPE_PACK).transpose(0, 2, 1)
    pos = jnp.repeat(pos, MLA_ROPE // 2, axis=2)
    p_tok = p.reshape(DEPTH, t, PLE_DIM)

    h = x.reshape(t, D_MODEL)
    for i in range(DEPTH):
        j = i // 2
        if i % 2 == 0:
            h, q, k, v = _pre_mla_call(h, ffn_g, *pre_w, pos, mix_g, wd, q_g, wuq, kv_g,
                                       wukv, invf, i, j)
            w_o = mla_w_o
        else:
            h, q, k, v = _pre_fox_call(h, ffn_g, *pre_w, mix_g, fox_w, fox_b, i, j,
                                       batch, seq)
            w_o = fox_w_o
        casts = [(w_o, (j,)), (ffn_w_in, (i, 1)), (ffn_w_out, (i, 1)), (ple_w_gate, (i,)),
                 (ple_w_proj, (i,))]
        if i + 1 < DEPTH:
            casts += [(ffn_w_in, (i + 1, 0)), (ffn_w_out, (i + 1, 0))]
        attn, cast = _attn_call(q.reshape(batch, seq, -1), k.reshape(batch, seq, -1),
                                v.reshape(batch, seq, -1), i % 2 == 0, casts)
        wo_b, w_in_b, w_out_b, w_gate_b, w_proj_b = cast[:5]
        pre_w = tuple(cast[5:])
        h = _post_call(h, attn.reshape(t, -1), p_tok, wo_b, ffn_g, w_in_b, w_out_b, ple_g,
                       w_gate_b, w_proj_b, fin_g, i, final=(i == DEPTH - 1))
    return h.reshape(batch, seq, D_MODEL)
```

```python
import functools

import jax
import jax.numpy as jnp
from jax import lax
from jax.experimental import pallas as pl
from jax.experimental.pallas import tpu as pltpu

D_MODEL = 1024
DEPTH = 4
CHUNK = 64
D_FF = 2816
PLE_DIM = 256
HEADS = 8
MLA_NOPE = 128
MLA_ROPE = 64
MLA_V = 128
MLA_Q_LORA = 256
MLA_KV_LORA = 128
ROPE_THETA = 10000.0
FOX_HEAD_DIM = 128
NORM_EPS = 1e-6
NEG_INF = -1e30

F32 = jnp.float32
BF16 = jnp.bfloat16

LANES = 128
QK_DIM = 2 * LANES
V_DIM = 128
TM = 512
FFN_COLS = 256
TQ = 256
HEADS_PER_STEP = 4
ROW_SPLIT = 2
SCORE_LOOKAHEAD = 3
BF16_SUBLANES = 16
VT_ROWS = V_DIM + BF16_SUBLANES
LOG2E = 1.4426950408889634
ROPE_PACK = LANES // (MLA_ROPE // 2)
VMEM_LIMIT = 56 * 1024 * 1024


def _rms(x, g):
    return x * lax.rsqrt(jnp.mean(x * x, axis=-1, keepdims=True) + NORM_EPS) * g


def _sigmoid(x):
    return 1.0 / (1.0 + jnp.exp(-x))


def _lane_roll(x, shift):
    return x if shift % LANES == 0 else pltpu.roll(x, shift % LANES, 1)


ROW_GROUPS = [slice(s * (TM // ROW_SPLIT), (s + 1) * (TM // ROW_SPLIT)) for s in range(ROW_SPLIT)]


def _ffn_half_step(h, g, w_in_ref, w_out_ref, act_ref):
    xn = _rms(h, g).astype(BF16)
    for c in range(D_FF // FFN_COLS):
        lo = c * FFN_COLS
        gate = jnp.dot(xn, w_in_ref[:, lo:lo + FFN_COLS], preferred_element_type=F32)
        up = jnp.dot(xn, w_in_ref[:, D_FF + lo:D_FF + lo + FFN_COLS],
                     preferred_element_type=F32)
        act_ref[:, lo:lo + FFN_COLS] = (gate * _sigmoid(gate) * up).astype(BF16)
    ys = [jnp.dot(act_ref[r, :], w_out_ref[...], preferred_element_type=F32) for r in ROW_GROUPS]
    return [h[r] + 0.5 * y for r, y in zip(ROW_GROUPS, ys)]


def _resident(block_shape, index):
    return pl.BlockSpec(block_shape, lambda *_: index, pipeline_mode=pl.Buffered(1))


def _params(n_axes):
    return pltpu.CompilerParams(
        dimension_semantics=("arbitrary",) * n_axes, vmem_limit_bytes=VMEM_LIMIT)


def _ffn_specs(layer, half):
    return [
        _resident((None, None, 1, D_MODEL), (layer, half, 0, 0)),
        _resident((D_MODEL, 2 * D_FF), (0, 0)),
        _resident((D_FF, D_MODEL), (0, 0)),
    ]


def _pre_mla_kernel(h_ref, fg_ref, w_in_ref, w_out_ref, pos_ref, g_ref, wd_ref, qn_ref, wuq_ref,
                    kvn_ref, wukv_ref, invf_ref, ho_ref, q_ref, k_ref, v_ref, act_ref):
    scale = (MLA_NOPE + MLA_ROPE) ** -0.5 * LOG2E
    hs = _ffn_half_step(h_ref[...], fg_ref[...], w_in_ref, w_out_ref, act_ref)
    for r, h in zip(ROW_GROUPS, hs):
        ho_ref[r, :] = h

    ang = pos_ref[...].astype(F32) * invf_ref[...]
    cosv = jnp.cos(ang)
    sinv = jnp.sin(ang)
    half = MLA_ROPE // 2
    lane = lax.broadcasted_iota(jnp.int32, (TM // ROPE_PACK, LANES), 1)
    cos_rows, sin_up_rows, sin_dn_rows = [], [], []
    for a in range(ROPE_PACK):
        lo_shift = (-a * half) % LANES
        hi_shift = (half - a * half) % LANES
        cos_lo, cos_hi = _lane_roll(cosv, lo_shift), _lane_roll(cosv, hi_shift)
        sin_lo, sin_hi = _lane_roll(sinv, lo_shift), _lane_roll(sinv, hi_shift)
        cos_rows.append(jnp.where(lane < half, cos_lo, jnp.where(lane < MLA_ROPE, cos_hi, 0.0)))
        sin_up_rows.append(jnp.where((lane >= half) & (lane < MLA_ROPE), sin_hi, 0.0))
        sin_dn_rows.append(jnp.where(lane < half, -sin_lo, 0.0))
    cos_t = jnp.concatenate(cos_rows, axis=0)
    sin_up = jnp.concatenate(sin_up_rows, axis=0)
    sin_dn = jnp.concatenate(sin_dn_rows, axis=0)

    rows = ROW_GROUPS
    cs = [jnp.dot(_rms(h, g_ref[...]).astype(BF16), wd_ref[...],
                  preferred_element_type=F32) for h in hs]
    lins = []
    for c in cs:
        cq = _rms(c[:, :MLA_Q_LORA], qn_ref[...]).astype(BF16)
        ckv = _rms(c[:, MLA_Q_LORA:MLA_Q_LORA + MLA_KV_LORA], kvn_ref[...]).astype(BF16)
        lins.append((jnp.dot(cq, wuq_ref[...], preferred_element_type=F32),
                     jnp.dot(ckv, wukv_ref[...], preferred_element_type=F32)))
    for r, c, (q_lin, kv) in zip(rows, cs, lins):
        def rope(x):
            return (x * cos_t[r] + pltpu.roll(x, half, 1) * sin_up[r]
                    + pltpu.roll(x, LANES - half, 1) * sin_dn[r])

        k_rope = rope(c[:, MLA_Q_LORA + MLA_KV_LORA:]).astype(BF16)
        for hh in range(HEADS):
            q_nope = q_lin[:, hh * LANES:(hh + 1) * LANES]
            q_rot = q_lin[:, (HEADS + hh) * LANES:(HEADS + hh + 1) * LANES]
            q_ref[r, hh * QK_DIM:hh * QK_DIM + LANES] = (q_nope * scale).astype(BF16)
            q_ref[r, hh * QK_DIM + LANES:(hh + 1) * QK_DIM] = (rope(q_rot) * scale).astype(BF16)
            k_ref[r, hh * QK_DIM:hh * QK_DIM + LANES] = kv[:, hh * 256:hh * 256 + 128].astype(BF16)
            k_ref[r, hh * QK_DIM + LANES:(hh + 1) * QK_DIM] = k_rope
            v_ref[r, hh * V_DIM:(hh + 1) * V_DIM] = (
                kv[:, hh * 256 + 128:(hh + 1) * 256].astype(BF16))


def _pre_mla_call(h, ffn_norm, w_in, w_out, pos, mix_norm, wd, qn, wuq, kvn, wukv, invf, layer, j):
    t = h.shape[0]
    tok = lambda w: pl.BlockSpec((TM, w), lambda i: (i, 0))
    return pl.pallas_call(
        _pre_mla_kernel,
        grid=(t // TM,),
        in_specs=[
            tok(D_MODEL),
            *_ffn_specs(layer, 0),
            pl.BlockSpec((None, TM // ROPE_PACK, LANES), lambda i: (i, 0, 0)),
            _resident((None, 1, D_MODEL), (layer, 0, 0)),
            _resident((None, D_MODEL, 512), (j, 0, 0)),
            _resident((None, 1, MLA_Q_LORA), (j, 0, 0)),
            _resident((None, MLA_Q_LORA, 2 * HEADS * LANES), (j, 0, 0)),
            _resident((None, 1, MLA_KV_LORA), (j, 0, 0)),
            _resident((None, MLA_KV_LORA, HEADS * (MLA_NOPE + MLA_V)), (j, 0, 0)),
            _resident((1, LANES), (0, 0)),
        ],
        out_specs=[tok(D_MODEL), tok(HEADS * QK_DIM), tok(HEADS * QK_DIM), tok(HEADS * V_DIM)],
        out_shape=[
            jax.ShapeDtypeStruct((t, D_MODEL), F32),
            jax.ShapeDtypeStruct((t, HEADS * QK_DIM), BF16),
            jax.ShapeDtypeStruct((t, HEADS * QK_DIM), BF16),
            jax.ShapeDtypeStruct((t, HEADS * V_DIM), BF16),
        ],
        scratch_shapes=[pltpu.VMEM((TM, D_FF), BF16)],
        compiler_params=_params(1),
        name="pre_mla",
    )(h, ffn_norm, w_in, w_out, pos, mix_norm, wd, qn, wuq, kvn, wukv, invf)


def _split3(x):
    hi = x.astype(BF16).astype(F32)
    r = x - hi
    mid = r.astype(BF16).astype(F32)
    lo = r - mid
    return hi, mid, lo


def _pre_fox_kernel(h_ref, fg_ref, w_in_ref, w_out_ref, g_ref, w_ref, bf_ref,
                    ho_ref, q_ref, k_ref, v_ref, act_ref, carry_ref, tri_ref):
    hd = HEADS * FOX_HEAD_DIM
    scale = FOX_HEAD_DIM ** -0.5 * LOG2E

    n_rows = TM // ROW_SPLIT

    @pl.when((pl.program_id(0) == 0) & (pl.program_id(1) == 0))
    def _():
        r = lax.broadcasted_iota(jnp.int32, (n_rows, n_rows), 0)
        cc = lax.broadcasted_iota(jnp.int32, (n_rows, n_rows), 1)
        tri_ref[...] = (cc <= r).astype(BF16)

    @pl.when(pl.program_id(1) == 0)
    def _():
        carry_ref[...] = jnp.zeros_like(carry_ref)

    hs = _ffn_half_step(h_ref[...], fg_ref[...], w_in_ref, w_out_ref, act_ref)
    lane = lax.broadcasted_iota(jnp.int32, (n_rows, LANES), 1)
    carry = carry_ref[...]
    for r, h in zip(ROW_GROUPS, hs):
        ho_ref[r, :] = h

        u = _rms(h, g_ref[...]).astype(BF16)
        f_logit = jnp.dot(u, w_ref[:, 3 * hd:], preferred_element_type=F32)
        q = jnp.dot(u, w_ref[:, :hd], preferred_element_type=F32)
        z = f_logit + bf_ref[...]
        log_f = -(jnp.maximum(-z, 0.0) + jnp.log1p(jnp.exp(-jnp.abs(z))))

        hi, mid, lo = _split3(log_f)
        packed = jnp.where(lane < HEADS, hi,
                           jnp.where(lane < 2 * HEADS, _lane_roll(mid, HEADS),
                                     jnp.where(lane < 3 * HEADS, _lane_roll(lo, 2 * HEADS), 0.0)))
        k = jnp.dot(u, w_ref[:, hd:2 * hd], preferred_element_type=F32)
        psum = jnp.dot(tri_ref[...], packed.astype(BF16), preferred_element_type=F32)
        cum = ((_lane_roll(psum, -2 * HEADS) + _lane_roll(psum, -HEADS)) + psum) + carry
        carry = cum[n_rows - 1:n_rows, :]

        cum2 = cum * LOG2E
        parts = []
        for hh in range(HEADS):
            cb = jnp.broadcast_to(cum2[:, hh:hh + 1], (n_rows, LANES))
            c_hi, c_mid, c_lo = _split3(cb)
            parts.append(jnp.where((lane == 0) | (lane == 3), c_hi,
                                   jnp.where((lane == 1) | (lane == 4), c_mid, c_lo)))
            q_ex = jnp.where(lane < 3, parts[hh], jnp.where(lane < 6, 1.0, 0.0))
            q_ref[r, hh * QK_DIM:hh * QK_DIM + LANES] = (
                q[:, hh * LANES:(hh + 1) * LANES] * scale).astype(BF16)
            q_ref[r, hh * QK_DIM + LANES:(hh + 1) * QK_DIM] = q_ex.astype(BF16)
        v = jnp.dot(u, w_ref[:, 2 * hd:3 * hd], preferred_element_type=F32)
        for hh in range(HEADS):
            k_ex = jnp.where(lane < 3, 1.0, jnp.where(lane < 6, -parts[hh], 0.0))
            k_ref[r, hh * QK_DIM:hh * QK_DIM + LANES] = (
                k[:, hh * LANES:(hh + 1) * LANES].astype(BF16))
            k_ref[r, hh * QK_DIM + LANES:(hh + 1) * QK_DIM] = k_ex.astype(BF16)
        v_ref[r, :] = v.astype(BF16)
    carry_ref[...] = carry


def _pre_fox_call(h, ffn_norm, w_in, w_out, mix_norm, fox_w, b_f, layer, j, batch, seq):
    t = h.shape[0]
    nt = seq // TM
    tok = lambda w: pl.BlockSpec((TM, w), lambda b, i: (b * nt + i, 0))
    return pl.pallas_call(
        _pre_fox_kernel,
        grid=(batch, nt),
        in_specs=[
            tok(D_MODEL),
            *_ffn_specs(layer, 0),
            _resident((None, 1, D_MODEL), (layer, 0, 0)),
            _resident((None, D_MODEL, 3 * HEADS * FOX_HEAD_DIM + LANES), (j, 0, 0)),
            _resident((None, 1, LANES), (j, 0, 0)),
        ],
        out_specs=[tok(D_MODEL), tok(HEADS * QK_DIM), tok(HEADS * QK_DIM), tok(HEADS * V_DIM)],
        out_shape=[
            jax.ShapeDtypeStruct((t, D_MODEL), F32),
            jax.ShapeDtypeStruct((t, HEADS * QK_DIM), BF16),
            jax.ShapeDtypeStruct((t, HEADS * QK_DIM), BF16),
            jax.ShapeDtypeStruct((t, HEADS * V_DIM), BF16),
        ],
        scratch_shapes=[pltpu.VMEM((TM, D_FF), BF16), pltpu.VMEM((1, LANES), F32),
                        pltpu.VMEM((TM // ROW_SPLIT, TM // ROW_SPLIT), BF16)],
        compiler_params=_params(2),
        name="pre_fox",
    )(h, ffn_norm, w_in, w_out, mix_norm, fox_w, b_f)


def _attn_kernel(q_ref, k_ref, v_ref, *rest, chunk_causal, seq, n_cast):
    o_ref, vt_ref = rest[n_cast], rest[-1]
    for src, dst in zip(rest[:n_cast], rest[n_cast + 1:-1]):
        dst[...] = src[...].astype(BF16)

    key = lax.broadcasted_iota(jnp.int32, (TQ, TQ), 0)
    qry = lax.broadcasted_iota(jnp.int32, (TQ, TQ), 1)
    if chunk_causal:
        shift = CHUNK.bit_length() - 1
        allowed = (key >> shift) <= (qry >> shift)
    else:
        allowed = key <= qry
    nt = (((1,), (1,)), ((), ()))

    for hh in range(HEADS_PER_STEP):
        vt_ref[hh, 0:V_DIM, :] = (
            v_ref[:, hh * V_DIM:(hh + 1) * V_DIM].astype(F32).T.astype(BF16))
        vt_ref[hh, V_DIM:, :] = jnp.ones((VT_ROWS - V_DIM, seq), BF16)

    def scores(hh, i):
        q0 = i * TQ
        qk = slice(hh * QK_DIM, (hh + 1) * QK_DIM)
        q = q_ref[q0:q0 + TQ, qk]
        s_dg = lax.dot_general(k_ref[q0:q0 + TQ, qk], q, nt, preferred_element_type=F32)
        s_dg = jnp.where(allowed, s_dg, NEG_INF)
        s_lo = None
        if i > 0:
            s_lo = lax.dot_general(k_ref[0:q0, qk], q, nt, preferred_element_type=F32)
        return s_dg, s_lo

    def finish(hh, i, s_dg, s_lo):
        q0 = i * TQ
        m = jnp.max(s_dg, axis=0, keepdims=True)
        if i > 0:
            m = jnp.maximum(m, jnp.max(s_lo, axis=0, keepdims=True))
        o_t = jnp.dot(vt_ref[hh, :, q0:q0 + TQ], jnp.exp2(s_dg - m).astype(BF16),
                      preferred_element_type=F32)
        if i > 0:
            o_t = o_t + jnp.dot(vt_ref[hh, :, 0:q0], jnp.exp2(s_lo - m).astype(BF16),
                                preferred_element_type=F32)
        out = o_t[0:V_DIM, :] / o_t[V_DIM:V_DIM + 1, :]
        o_ref[q0:q0 + TQ, hh * V_DIM:(hh + 1) * V_DIM] = out.T.astype(o_ref.dtype)

    items = [(hh, i) for hh in range(HEADS_PER_STEP) for i in range(seq // TQ)]
    pending = [scores(*it) for it in items[:SCORE_LOOKAHEAD]]
    for n, item in enumerate(items):
        if n + SCORE_LOOKAHEAD < len(items):
            pending.append(scores(*items[n + SCORE_LOOKAHEAD]))
        finish(*item, *pending.pop(0))


def _attn_call(q, k, v, chunk_causal, casts):
    batch, seq, _ = q.shape
    hsteps = HEADS // HEADS_PER_STEP
    steps = batch * hsteps
    cast_in, cast_specs_in, cast_specs_out, cast_shapes = [], [], [], []
    for w, lead in casts:
        n_rows, cols = w.shape[-2:]
        rows = n_rows // steps
        assert rows * steps == n_rows and rows % BF16_SUBLANES == 0
        cast_in.append(w)
        cast_specs_in.append(pl.BlockSpec(
            (None,) * len(lead) + (rows, cols),
            lambda b, h, lead=lead: (*lead, b * hsteps + h, 0)))
        cast_specs_out.append(pl.BlockSpec((rows, cols), lambda b, h: (b * hsteps + h, 0)))
        cast_shapes.append(jax.ShapeDtypeStruct((n_rows, cols), BF16))
    n_cast = len(cast_in)
    head_block = lambda w: pl.BlockSpec((None, seq, HEADS_PER_STEP * w), lambda b, h: (b, 0, h))
    outs = pl.pallas_call(
        functools.partial(_attn_kernel, chunk_causal=chunk_causal, seq=seq, n_cast=n_cast),
        grid=(batch, hsteps),
        in_specs=[head_block(QK_DIM), head_block(QK_DIM), head_block(V_DIM), *cast_specs_in],
        out_specs=[head_block(V_DIM), *cast_specs_out],
        out_shape=[jax.ShapeDtypeStruct((batch, seq, HEADS * V_DIM), BF16), *cast_shapes],
        scratch_shapes=[pltpu.VMEM((HEADS_PER_STEP, VT_ROWS, seq), BF16)],
        compiler_params=_params(2),
        name="attn",
    )(q, k, v, *cast_in)
    return outs[0], list(outs[1:])


def _post_kernel(h_ref, a_ref, p_ref, wo_ref, g_ref, w_in_ref, w_out_ref, pg_ref, wg_ref,
                 wp_ref, fn_ref, o_ref, act_ref, *, final):
    h = h_ref[...] + jnp.dot(a_ref[...], wo_ref[...], preferred_element_type=F32)
    hs = _ffn_half_step(h, g_ref[...], w_in_ref, w_out_ref, act_ref)
    for r, h in zip(ROW_GROUPS, hs):
        gate = _sigmoid(jnp.dot(_rms(h, pg_ref[...]).astype(BF16), wg_ref[...],
                                preferred_element_type=F32))
        emb = jnp.dot(p_ref[r, :].astype(BF16), wp_ref[...], preferred_element_type=F32)
        h = h + gate * emb
        if final:
            h = _rms(h, fn_ref[...])
        o_ref[r, :] = h


def _post_call(h, attn, p, w_o, ffn_norm, w_in, w_out, ple_norm, w_gate, w_proj, final_norm,
               layer, final):
    t = h.shape[0]
    return pl.pallas_call(
        functools.partial(_post_kernel, final=final),
        grid=(t // TM,),
        in_specs=[
            pl.BlockSpec((TM, D_MODEL), lambda i: (i, 0)),
            pl.BlockSpec((TM, HEADS * V_DIM), lambda i: (i, 0)),
            pl.BlockSpec((None, TM, PLE_DIM), lambda i: (layer, i, 0)),
            _resident((HEADS * V_DIM, D_MODEL), (0, 0)),
            *_ffn_specs(layer, 1),
            _resident((None, 1, D_MODEL), (layer, 0, 0)),
            _resident((D_MODEL, D_MODEL), (0, 0)),
            _resident((PLE_DIM, D_MODEL), (0, 0)),
            _resident((1, D_MODEL), (0, 0)),
        ],
        out_specs=pl.BlockSpec((TM, D_MODEL), lambda i: (i, 0)),
        out_shape=jax.ShapeDtypeStruct((t, D_MODEL), F32),
        scratch_shapes=[pltpu.VMEM((TM, D_FF), BF16)],
        compiler_params=_params(1),
        name="post",
    )(h, attn, p, w_o, ffn_norm, w_in, w_out, ple_norm, w_gate, w_proj, final_norm)


def kernel(x, p, positions, ffn_norm, ffn_w_in, ffn_w_out, mix_norm, mla_w_down, mla_q_norm,
           mla_w_uq, mla_kv_norm, mla_w_ukv, mla_w_o, fox_w_in, fox_b_f, fox_w_o, ple_norm,
           ple_w_gate, ple_w_proj, final_norm):
    batch, seq, _ = x.shape
    t = batch * seq
    n_mla = mla_w_down.shape[0]
    n_fox = fox_w_in.shape[0]

    pre_w = (ffn_w_in[0, 0].astype(BF16), ffn_w_out[0, 0].astype(BF16))
    ffn_g = ffn_norm.reshape(DEPTH, 2, 1, D_MODEL)
    mix_g = mix_norm.reshape(DEPTH, 1, D_MODEL)
    ple_g = ple_norm.reshape(DEPTH, 1, D_MODEL)
    fin_g = final_norm.reshape(1, D_MODEL)

    wd = jnp.pad(mla_w_down, ((0, 0), (0, 0), (0, LANES - MLA_ROPE))).astype(BF16)
    wq = mla_w_uq.reshape(n_mla, MLA_Q_LORA, HEADS, MLA_NOPE + MLA_ROPE)
    wq_nope = wq[..., :MLA_NOPE].reshape(n_mla, MLA_Q_LORA, HEADS * LANES)
    wq_rope = jnp.pad(wq[..., MLA_NOPE:], ((0, 0), (0, 0), (0, 0), (0, LANES - MLA_ROPE)))
    wuq = jnp.concatenate(
        [wq_nope, wq_rope.reshape(n_mla, MLA_Q_LORA, HEADS * LANES)], axis=-1).astype(BF16)
    wukv = mla_w_ukv.astype(BF16)
    q_g = mla_q_norm.reshape(n_mla, 1, MLA_Q_LORA)
    kv_g = mla_kv_norm.reshape(n_mla, 1, MLA_KV_LORA)
    inv_freq = ROPE_THETA ** (-jnp.arange(0, MLA_ROPE, 2, dtype=F32) / MLA_ROPE)
    invf = jnp.tile(inv_freq, ROPE_PACK)[None, :]

    fox_w = jnp.pad(fox_w_in, ((0, 0), (0, 0), (0, LANES - HEADS))).astype(BF16)
    fox_b = jnp.pad(fox_b_f, ((0, 0), (0, LANES - HEADS))).reshape(n_fox, 1, LANES)

    pos = positions.reshape(t // TM, ROPE_PACK, TM // ROPE_PACK).transpose(0, 2, 1)
    pos = jnp.repeat(pos, MLA_ROPE // 2, axis=2)
    p_tok = p.reshape(DEPTH, t, PLE_DIM)

    h = x.reshape(t, D_MODEL)
    for i in range(DEPTH):
        j = i // 2
        if i % 2 == 0:
            h, q, k, v = _pre_mla_call(h, ffn_g, *pre_w, pos, mix_g, wd, q_g, wuq, kv_g,
                                       wukv, invf, i, j)
            w_o = mla_w_o
        else:
            h, q, k, v = _pre_fox_call(h, ffn_g, *pre_w, mix_g, fox_w, fox_b, i, j,
                                       batch, seq)
            w_o = fox_w_o
        casts = [(w_o, (j,)), (ffn_w_in, (i, 1)), (ffn_w_out, (i, 1)), (ple_w_gate, (i,)),
                 (ple_w_proj, (i,))]
        if i + 1 < DEPTH:
            casts += [(ffn_w_in, (i + 1, 0)), (ffn_w_out, (i + 1, 0))]
        attn, cast = _attn_call(q.reshape(batch, seq, -1), k.reshape(batch, seq, -1),
                                v.reshape(batch, seq, -1), i % 2 == 0, casts)
        wo_b, w_in_b, w_out_b, w_gate_b, w_proj_b = cast[:5]
        pre_w = tuple(cast[5:])
        h = _post_call(h, attn.reshape(t, -1), p_tok, wo_b, ffn_g, w_in_b, w_out_b, ple_g,
                       w_gate_b, w_proj_b, fin_g, i, final=(i == DEPTH - 1))
    return h.reshape(batch, seq, D_MODEL)
```

```python
import functools

import jax
import jax.numpy as jnp
from jax import lax
from jax.experimental import pallas as pl
from jax.experimental.pallas import tpu as pltpu

D_MODEL = 1024
DEPTH = 4
CHUNK = 64
D_FF = 2816
PLE_DIM = 256
HEADS = 8
MLA_NOPE = 128
MLA_ROPE = 64
MLA_V = 128
MLA_Q_LORA = 256
MLA_KV_LORA = 128
ROPE_THETA = 10000.0
FOX_HEAD_DIM = 128
NORM_EPS = 1e-6
NEG_INF = -1e30

F32 = jnp.float32
BF16 = jnp.bfloat16

LANES = 128
QK_DIM = 2 * LANES
V_DIM = 128
TM = 512
FFN_COLS = 256
TQ = 256
HEADS_PER_STEP = 4
ROW_SPLIT = 2
SCORE_LOOKAHEAD = 3
BF16_SUBLANES = 16
VT_ROWS = V_DIM + BF16_SUBLANES
LOG2E = 1.4426950408889634
ROPE_PACK = LANES // (MLA_ROPE // 2)
VMEM_LIMIT = 56 * 1024 * 1024


def _rms(x, g):
    return x * lax.rsqrt(jnp.mean(x * x, axis=-1, keepdims=True) + NORM_EPS) * g


def _sigmoid(x):
    return 1.0 / (1.0 + jnp.exp(-x))


def _lane_roll(x, shift):
    return x if shift % LANES == 0 else pltpu.roll(x, shift % LANES, 1)


ROW_GROUPS = [slice(s * (TM // ROW_SPLIT), (s + 1) * (TM // ROW_SPLIT)) for s in range(ROW_SPLIT)]


def _ffn_half_step(h, g, w_in_ref, w_out_ref, act_ref):
    xn = _rms(h, g).astype(BF16)
    for c in range(D_FF // FFN_COLS):
        lo = c * FFN_COLS
        gate = jnp.dot(xn, w_in_ref[:, lo:lo + FFN_COLS], preferred_element_type=F32)
        up = jnp.dot(xn, w_in_ref[:, D_FF + lo:D_FF + lo + FFN_COLS],
                     preferred_element_type=F32)
        act_ref[:, lo:lo + FFN_COLS] = (gate * _sigmoid(gate) * up).astype(BF16)
    ys = [jnp.dot(act_ref[r, :], w_out_ref[...], preferred_element_type=F32) for r in ROW_GROUPS]
    return [h[r] + 0.5 * y for r, y in zip(ROW_GROUPS, ys)]


def _resident(block_shape, index):
    return pl.BlockSpec(block_shape, lambda *_: index, pipeline_mode=pl.Buffered(1))


def _params(n_axes):
    return pltpu.CompilerParams(
        dimension_semantics=("arbitrary",) * n_axes, vmem_limit_bytes=VMEM_LIMIT)


def _ffn_specs(layer, half):
    return [
        _resident((None, None, 1, D_MODEL), (layer, half, 0, 0)),
        _resident((D_MODEL, 2 * D_FF), (0, 0)),
        _resident((D_FF, D_MODEL), (0, 0)),
    ]


def _pre_mla_kernel(h_ref, fg_ref, w_in_ref, w_out_ref, pos_ref, g_ref, wd_ref, qn_ref, wuq_ref,
                    kvn_ref, wukv_ref, invf_ref, ho_ref, q_ref, k_ref, v_ref, act_ref):
    scale = (MLA_NOPE + MLA_ROPE) ** -0.5 * LOG2E
    hs = _ffn_half_step(h_ref[...], fg_ref[...], w_in_ref, w_out_ref, act_ref)
    for r, h in zip(ROW_GROUPS, hs):
        ho_ref[r, :] = h

    ang = pos_ref[...].astype(F32) * invf_ref[...]
    cosv = jnp.cos(ang)
    sinv = jnp.sin(ang)
    half = MLA_ROPE // 2
    lane = lax.broadcasted_iota(jnp.int32, (TM // ROPE_PACK, LANES), 1)
    cos_rows, sin_rows = [], []
    for a in range(ROPE_PACK):
        lo_shift = (-a * half) % LANES
        hi_shift = (half - a * half) % LANES

        def table(v):
            return jnp.where(lane < half, _lane_roll(v, lo_shift),
                             jnp.where(lane < MLA_ROPE, _lane_roll(v, hi_shift), 0.0))

        cos_rows.append(table(cosv))
        sin_rows.append(table(sinv))
    cos_t = jnp.concatenate(cos_rows, axis=0)
    sin_t = jnp.concatenate(sin_rows, axis=0)

    rows = ROW_GROUPS
    cs = [jnp.dot(_rms(h, g_ref[...]).astype(BF16), wd_ref[...],
                  preferred_element_type=F32) for h in hs]
    lins = []
    for c in cs:
        cq = _rms(c[:, :MLA_Q_LORA], qn_ref[...]).astype(BF16)
        ckv = _rms(c[:, MLA_Q_LORA:MLA_Q_LORA + MLA_KV_LORA], kvn_ref[...]).astype(BF16)
        lins.append((jnp.dot(cq, wuq_ref[...], preferred_element_type=F32),
                     jnp.dot(ckv, wukv_ref[...], preferred_element_type=F32)))
    for r, c, (q_lin, kv) in zip(rows, cs, lins):
        def rope(x):
            return x * cos_t[r] + pltpu.roll(x, MLA_ROPE, 1) * sin_t[r]

        k_rope = rope(c[:, MLA_Q_LORA + MLA_KV_LORA:]).astype(BF16)
        for hh in range(HEADS):
            q_nope = q_lin[:, hh * LANES:(hh + 1) * LANES]
            q_rot = q_lin[:, (HEADS + hh) * LANES:(HEADS + hh + 1) * LANES]
            q_ref[r, hh * QK_DIM:hh * QK_DIM + LANES] = (q_nope * scale).astype(BF16)
            q_ref[r, hh * QK_DIM + LANES:(hh + 1) * QK_DIM] = (rope(q_rot) * scale).astype(BF16)
            k_ref[r, hh * QK_DIM:hh * QK_DIM + LANES] = kv[:, hh * 256:hh * 256 + 128].astype(BF16)
            k_ref[r, hh * QK_DIM + LANES:(hh + 1) * QK_DIM] = k_rope
            v_ref[r, hh * V_DIM:(hh + 1) * V_DIM] = (
                kv[:, hh * 256 + 128:(hh + 1) * 256].astype(BF16))


def _pre_mla_call(h, ffn_norm, w_in, w_out, pos, mix_norm, wd, qn, wuq, kvn, wukv, invf, layer, j):
    t = h.shape[0]
    tok = lambda w: pl.BlockSpec((TM, w), lambda i: (i, 0))
    return pl.pallas_call(
        _pre_mla_kernel,
        grid=(t // TM,),
        in_specs=[
            tok(D_MODEL),
            *_ffn_specs(layer, 0),
            pl.BlockSpec((None, TM // ROPE_PACK, LANES), lambda i: (i, 0, 0)),
            _resident((None, 1, D_MODEL), (layer, 0, 0)),
            _resident((None, D_MODEL, 512), (j, 0, 0)),
            _resident((None, 1, MLA_Q_LORA), (j, 0, 0)),
            _resident((None, MLA_Q_LORA, 2 * HEADS * LANES), (j, 0, 0)),
            _resident((None, 1, MLA_KV_LORA), (j, 0, 0)),
            _resident((None, MLA_KV_LORA, HEADS * (MLA_NOPE + MLA_V)), (j, 0, 0)),
            _resident((1, LANES), (0, 0)),
        ],
        out_specs=[tok(D_MODEL), tok(HEADS * QK_DIM), tok(HEADS * QK_DIM), tok(HEADS * V_DIM)],
        out_shape=[
            jax.ShapeDtypeStruct((t, D_MODEL), F32),
            jax.ShapeDtypeStruct((t, HEADS * QK_DIM), BF16),
            jax.ShapeDtypeStruct((t, HEADS * QK_DIM), BF16),
            jax.ShapeDtypeStruct((t, HEADS * V_DIM), BF16),
        ],
        scratch_shapes=[pltpu.VMEM((TM, D_FF), BF16)],
        compiler_params=_params(1),
        name="pre_mla",
    )(h, ffn_norm, w_in, w_out, pos, mix_norm, wd, qn, wuq, kvn, wukv, invf)


def _split3(x):
    hi = x.astype(BF16).astype(F32)
    r = x - hi
    mid = r.astype(BF16).astype(F32)
    lo = r - mid
    return hi, mid, lo


def _pre_fox_kernel(h_ref, fg_ref, w_in_ref, w_out_ref, g_ref, w_ref, bf_ref,
                    ho_ref, q_ref, k_ref, v_ref, act_ref, carry_ref, tri_ref):
    hd = HEADS * FOX_HEAD_DIM
    scale = FOX_HEAD_DIM ** -0.5 * LOG2E

    n_rows = TM // ROW_SPLIT

    @pl.when((pl.program_id(0) == 0) & (pl.program_id(1) == 0))
    def _():
        r = lax.broadcasted_iota(jnp.int32, (n_rows, n_rows), 0)
        cc = lax.broadcasted_iota(jnp.int32, (n_rows, n_rows), 1)
        tri_ref[...] = (cc <= r).astype(BF16)

    @pl.when(pl.program_id(1) == 0)
    def _():
        carry_ref[...] = jnp.zeros_like(carry_ref)

    hs = _ffn_half_step(h_ref[...], fg_ref[...], w_in_ref, w_out_ref, act_ref)
    lane = lax.broadcasted_iota(jnp.int32, (n_rows, LANES), 1)
    carry = carry_ref[...]
    for r, h in zip(ROW_GROUPS, hs):
        ho_ref[r, :] = h

        u = _rms(h, g_ref[...]).astype(BF16)
        f_logit = jnp.dot(u, w_ref[:, 3 * hd:], preferred_element_type=F32)
        q = jnp.dot(u, w_ref[:, :hd], preferred_element_type=F32)
        z = f_logit + bf_ref[...]
        log_f = -(jnp.maximum(-z, 0.0) + jnp.log1p(jnp.exp(-jnp.abs(z))))

        hi, mid, lo = _split3(log_f)
        packed = jnp.where(lane < HEADS, hi,
                           jnp.where(lane < 2 * HEADS, _lane_roll(mid, HEADS),
                                     jnp.where(lane < 3 * HEADS, _lane_roll(lo, 2 * HEADS), 0.0)))
        k = jnp.dot(u, w_ref[:, hd:2 * hd], preferred_element_type=F32)
        psum = jnp.dot(tri_ref[...], packed.astype(BF16), preferred_element_type=F32)
        cum = ((_lane_roll(psum, -2 * HEADS) + _lane_roll(psum, -HEADS)) + psum) + carry
        carry = cum[n_rows - 1:n_rows, :]

        cum2 = cum * LOG2E
        parts = []
        for hh in range(HEADS):
            cb = jnp.broadcast_to(cum2[:, hh:hh + 1], (n_rows, LANES))
            c_hi, c_mid, c_lo = _split3(cb)
            parts.append(jnp.where((lane == 0) | (lane == 3), c_hi,
                                   jnp.where((lane == 1) | (lane == 4), c_mid, c_lo)))
            q_ex = jnp.where(lane < 3, parts[hh], jnp.where(lane < 6, 1.0, 0.0))
            q_ref[r, hh * QK_DIM:hh * QK_DIM + LANES] = (
                q[:, hh * LANES:(hh + 1) * LANES] * scale).astype(BF16)
            q_ref[r, hh * QK_DIM + LANES:(hh + 1) * QK_DIM] = q_ex.astype(BF16)
        v = jnp.dot(u, w_ref[:, 2 * hd:3 * hd], preferred_element_type=F32)
        for hh in range(HEADS):
            k_ex = jnp.where(lane < 3, 1.0, jnp.where(lane < 6, -parts[hh], 0.0))
            k_ref[r, hh * QK_DIM:hh * QK_DIM + LANES] = (
                k[:, hh * LANES:(hh + 1) * LANES].astype(BF16))
            k_ref[r, hh * QK_DIM + LANES:(hh + 1) * QK_DIM] = k_ex.astype(BF16)
        v_ref[r, :] = v.astype(BF16)
    carry_ref[...] = carry


def _pre_fox_call(h, ffn_norm, w_in, w_out, mix_norm, fox_w, b_f, layer, j, batch, seq):
    t = h.shape[0]
    nt = seq // TM
    tok = lambda w: pl.BlockSpec((TM, w), lambda b, i: (b * nt + i, 0))
    return pl.pallas_call(
        _pre_fox_kernel,
        grid=(batch, nt),
        in_specs=[
            tok(D_MODEL),
            *_ffn_specs(layer, 0),
            _resident((None, 1, D_MODEL), (layer, 0, 0)),
            _resident((None, D_MODEL, 3 * HEADS * FOX_HEAD_DIM + LANES), (j, 0, 0)),
            _resident((None, 1, LANES), (j, 0, 0)),
        ],
        out_specs=[tok(D_MODEL), tok(HEADS * QK_DIM), tok(HEADS * QK_DIM), tok(HEADS * V_DIM)],
        out_shape=[
            jax.ShapeDtypeStruct((t, D_MODEL), F32),
            jax.ShapeDtypeStruct((t, HEADS * QK_DIM), BF16),
            jax.ShapeDtypeStruct((t, HEADS * QK_DIM), BF16),
            jax.ShapeDtypeStruct((t, HEADS * V_DIM), BF16),
        ],
        scratch_shapes=[pltpu.VMEM((TM, D_FF), BF16), pltpu.VMEM((1, LANES), F32),
                        pltpu.VMEM((TM // ROW_SPLIT, TM // ROW_SPLIT), BF16)],
        compiler_params=_params(2),
        name="pre_fox",
    )(h, ffn_norm, w_in, w_out, mix_norm, fox_w, b_f)


def _attn_kernel(q_ref, k_ref, v_ref, *rest, chunk_causal, seq, n_cast):
    o_ref, vt_ref = rest[n_cast], rest[-1]
    for src, dst in zip(rest[:n_cast], rest[n_cast + 1:-1]):
        dst[...] = src[...].astype(BF16)

    key = lax.broadcasted_iota(jnp.int32, (TQ, TQ), 0)
    qry = lax.broadcasted_iota(jnp.int32, (TQ, TQ), 1)
    if chunk_causal:
        shift = CHUNK.bit_length() - 1
        allowed = (key >> shift) <= (qry >> shift)
    else:
        allowed = key <= qry
    nt = (((1,), (1,)), ((), ()))

    for hh in range(HEADS_PER_STEP):
        vt_ref[hh, 0:V_DIM, :] = (
            v_ref[:, hh * V_DIM:(hh + 1) * V_DIM].astype(F32).T.astype(BF16))
        vt_ref[hh, V_DIM:, :] = jnp.ones((VT_ROWS - V_DIM, seq), BF16)

    def scores(hh, i):
        q0 = i * TQ
        qk = slice(hh * QK_DIM, (hh + 1) * QK_DIM)
        q = q_ref[q0:q0 + TQ, qk]
        s_dg = lax.dot_general(k_ref[q0:q0 + TQ, qk], q, nt, preferred_element_type=F32)
        s_dg = jnp.where(allowed, s_dg, NEG_INF)
        s_lo = None
        if i > 0:
            s_lo = lax.dot_general(k_ref[0:q0, qk], q, nt, preferred_element_type=F32)
        return s_dg, s_lo

    def finish(hh, i, s_dg, s_lo):
        q0 = i * TQ
        m = jnp.max(s_dg, axis=0, keepdims=True)
        if i > 0:
            m = jnp.maximum(m, jnp.max(s_lo, axis=0, keepdims=True))
        o_t = jnp.dot(vt_ref[hh, :, q0:q0 + TQ], jnp.exp2(s_dg - m).astype(BF16),
                      preferred_element_type=F32)
        if i > 0:
            o_t = o_t + jnp.dot(vt_ref[hh, :, 0:q0], jnp.exp2(s_lo - m).astype(BF16),
                                preferred_element_type=F32)
        out = o_t[0:V_DIM, :] / o_t[V_DIM:V_DIM + 1, :]
        o_ref[q0:q0 + TQ, hh * V_DIM:(hh + 1) * V_DIM] = out.T.astype(o_ref.dtype)

    items = [(hh, i) for hh in range(HEADS_PER_STEP) for i in range(seq // TQ)]
    pending = [scores(*it) for it in items[:SCORE_LOOKAHEAD]]
    for n, item in enumerate(items):
        if n + SCORE_LOOKAHEAD < len(items):
            pending.append(scores(*items[n + SCORE_LOOKAHEAD]))
        finish(*item, *pending.pop(0))


def _attn_call(q, k, v, chunk_causal, casts):
    batch, seq, _ = q.shape
    hsteps = HEADS // HEADS_PER_STEP
    steps = batch * hsteps
    cast_in, cast_specs_in, cast_specs_out, cast_shapes = [], [], [], []
    for w, lead in casts:
        n_rows, cols = w.shape[-2:]
        rows = n_rows // steps
        assert rows * steps == n_rows and rows % BF16_SUBLANES == 0
        cast_in.append(w)
        cast_specs_in.append(pl.BlockSpec(
            (None,) * len(lead) + (rows, cols),
            lambda b, h, lead=lead: (*lead, b * hsteps + h, 0)))
        cast_specs_out.append(pl.BlockSpec((rows, cols), lambda b, h: (b * hsteps + h, 0)))
        cast_shapes.append(jax.ShapeDtypeStruct((n_rows, cols), BF16))
    n_cast = len(cast_in)
    head_block = lambda w: pl.BlockSpec((None, seq, HEADS_PER_STEP * w), lambda b, h: (b, 0, h))
    outs = pl.pallas_call(
        functools.partial(_attn_kernel, chunk_causal=chunk_causal, seq=seq, n_cast=n_cast),
        grid=(batch, hsteps),
        in_specs=[head_block(QK_DIM), head_block(QK_DIM), head_block(V_DIM), *cast_specs_in],
        out_specs=[head_block(V_DIM), *cast_specs_out],
        out_shape=[jax.ShapeDtypeStruct((batch, seq, HEADS * V_DIM), BF16), *cast_shapes],
        scratch_shapes=[pltpu.VMEM((HEADS_PER_STEP, VT_ROWS, seq), BF16)],
        compiler_params=_params(2),
        name="attn",
    )(q, k, v, *cast_in)
    return outs[0], list(outs[1:])


def _post_kernel(h_ref, a_ref, p_ref, wo_ref, g_ref, w_in_ref, w_out_ref, pg_ref, wg_ref,
                 wp_ref, fn_ref, o_ref, act_ref, *, final):
    h = h_ref[...] + jnp.dot(a_ref[...], wo_ref[...], preferred_element_type=F32)
    hs = _ffn_half_step(h, g_ref[...], w_in_ref, w_out_ref, act_ref)
    for r, h in zip(ROW_GROUPS, hs):
        gate = _sigmoid(jnp.dot(_rms(h, pg_ref[...]).astype(BF16), wg_ref[...],
                                preferred_element_type=F32))
        emb = jnp.dot(p_ref[r, :].astype(BF16), wp_ref[...], preferred_element_type=F32)
        h = h + gate * emb
        if final:
            h = _rms(h, fn_ref[...])
        o_ref[r, :] = h


def _post_call(h, attn, p, w_o, ffn_norm, w_in, w_out, ple_norm, w_gate, w_proj, final_norm,
               layer, final):
    t = h.shape[0]
    return pl.pallas_call(
        functools.partial(_post_kernel, final=final),
        grid=(t // TM,),
        in_specs=[
            pl.BlockSpec((TM, D_MODEL), lambda i: (i, 0)),
            pl.BlockSpec((TM, HEADS * V_DIM), lambda i: (i, 0)),
            pl.BlockSpec((None, TM, PLE_DIM), lambda i: (layer, i, 0)),
            _resident((HEADS * V_DIM, D_MODEL), (0, 0)),
            *_ffn_specs(layer, 1),
            _resident((None, 1, D_MODEL), (layer, 0, 0)),
            _resident((D_MODEL, D_MODEL), (0, 0)),
            _resident((PLE_DIM, D_MODEL), (0, 0)),
            _resident((1, D_MODEL), (0, 0)),
        ],
        out_specs=pl.BlockSpec((TM, D_MODEL), lambda i: (i, 0)),
        out_shape=jax.ShapeDtypeStruct((t, D_MODEL), F32),
        scratch_shapes=[pltpu.VMEM((TM, D_FF), BF16)],
        compiler_params=_params(1),
        name="post",
    )(h, attn, p, w_o, ffn_norm, w_in, w_out, ple_norm, w_gate, w_proj, final_norm)


def kernel(x, p, positions, ffn_norm, ffn_w_in, ffn_w_out, mix_norm, mla_w_down, mla_q_norm,
           mla_w_uq, mla_kv_norm, mla_w_ukv, mla_w_o, fox_w_in, fox_b_f, fox_w_o, ple_norm,
           ple_w_gate, ple_w_proj, final_norm):
    batch, seq, _ = x.shape
    t = batch * seq
    n_mla = mla_w_down.shape[0]
    n_fox = fox_w_in.shape[0]

    pre_w = (ffn_w_in[0, 0].astype(BF16), ffn_w_out[0, 0].astype(BF16))
    ffn_g = ffn_norm.reshape(DEPTH, 2, 1, D_MODEL)
    mix_g = mix_norm.reshape(DEPTH, 1, D_MODEL)
    ple_g = ple_norm.reshape(DEPTH, 1, D_MODEL)
    fin_g = final_norm.reshape(1, D_MODEL)

    def with_partner(w_rope):
        x1, x2 = w_rope[..., :MLA_ROPE // 2], w_rope[..., MLA_ROPE // 2:]
        return jnp.concatenate([w_rope, -x2, x1], axis=-1)

    kv_end = MLA_Q_LORA + MLA_KV_LORA
    wd = jnp.concatenate(
        [mla_w_down[..., :kv_end], with_partner(mla_w_down[..., kv_end:])], axis=-1).astype(BF16)
    wq = mla_w_uq.reshape(n_mla, MLA_Q_LORA, HEADS, MLA_NOPE + MLA_ROPE)
    wq_nope = wq[..., :MLA_NOPE].reshape(n_mla, MLA_Q_LORA, HEADS * LANES)
    wq_rope = with_partner(wq[..., MLA_NOPE:])
    wuq = jnp.concatenate(
        [wq_nope, wq_rope.reshape(n_mla, MLA_Q_LORA, HEADS * LANES)], axis=-1).astype(BF16)
    wukv = mla_w_ukv.astype(BF16)
    q_g = mla_q_norm.reshape(n_mla, 1, MLA_Q_LORA)
    kv_g = mla_kv_norm.reshape(n_mla, 1, MLA_KV_LORA)
    inv_freq = ROPE_THETA ** (-jnp.arange(0, MLA_ROPE, 2, dtype=F32) / MLA_ROPE)
    invf = jnp.tile(inv_freq, ROPE_PACK)[None, :]

    fox_w = jnp.pad(fox_w_in, ((0, 0), (0, 0), (0, LANES - HEADS))).astype(BF16)
    fox_b = jnp.pad(fox_b_f, ((0, 0), (0, LANES - HEADS))).reshape(n_fox, 1, LANES)

    pos = positions.reshape(t // TM, ROPE_PACK, TM // ROPE_PACK).transpose(0, 2, 1)
    pos = jnp.repeat(pos, MLA_ROPE // 2, axis=2)
    p_tok = p.reshape(DEPTH, t, PLE_DIM)

    h = x.reshape(t, D_MODEL)
    for i in range(DEPTH):
        j = i // 2
        if i % 2 == 0:
            h, q, k, v = _pre_mla_call(h, ffn_g, *pre_w, pos, mix_g, wd, q_g, wuq, kv_g,
                                       wukv, invf, i, j)
            w_o = mla_w_o
        else:
            h, q, k, v = _pre_fox_call(h, ffn_g, *pre_w, mix_g, fox_w, fox_b, i, j,
                                       batch, seq)
            w_o = fox_w_o
        casts = [(w_o, (j,)), (ffn_w_in, (i, 1)), (ffn_w_out, (i, 1)), (ple_w_gate, (i,)),
                 (ple_w_proj, (i,))]
        if i + 1 < DEPTH:
            casts += [(ffn_w_in, (i + 1, 0)), (ffn_w_out, (i + 1, 0))]
        attn, cast = _attn_call(q.reshape(batch, seq, -1), k.reshape(batch, seq, -1),
                                v.reshape(batch, seq, -1), i % 2 == 0, casts)
        wo_b, w_in_b, w_out_b, w_gate_b, w_proj_b = cast[:5]
        pre_w = tuple(cast[5:])
        h = _post_call(h, attn.reshape(t, -1), p_tok, wo_b, ffn_g, w_in_b, w_out_b, ple_g,
                       w_gate_b, w_proj_b, fin_g, i, final=(i == DEPTH - 1))
    return h.reshape(batch, seq, D_MODEL)
```

```python
import functools

import jax
import jax.numpy as jnp
from jax import lax
from jax.experimental import pallas as pl
from jax.experimental.pallas import tpu as pltpu

D_MODEL = 1024
DEPTH = 4
CHUNK = 64
D_FF = 2816
PLE_DIM = 256
HEADS = 8
MLA_NOPE = 128
MLA_ROPE = 64
MLA_V = 128
MLA_Q_LORA = 256
MLA_KV_LORA = 128
ROPE_THETA = 10000.0
FOX_HEAD_DIM = 128
NORM_EPS = 1e-6
NEG_INF = -1e30

F32 = jnp.float32
BF16 = jnp.bfloat16

LANES = 128
QK_DIM = 2 * LANES
V_DIM = 128
TM = 512
FFN_COLS = 256
TQ = 256
HEADS_PER_STEP = 4
ROW_SPLIT = 2
SCORE_LOOKAHEAD = 3
BF16_SUBLANES = 16
LOG2E = 1.4426950408889634
ROPE_PACK = LANES // (MLA_ROPE // 2)
VMEM_LIMIT = 56 * 1024 * 1024


def _rms(x, g):
    return x * lax.rsqrt(jnp.mean(x * x, axis=-1, keepdims=True) + NORM_EPS) * g


def _sigmoid(x):
    return 1.0 / (1.0 + jnp.exp(-x))


def _lane_roll(x, shift):
    return x if shift % LANES == 0 else pltpu.roll(x, shift % LANES, 1)


ROW_GROUPS = [slice(s * (TM // ROW_SPLIT), (s + 1) * (TM // ROW_SPLIT)) for s in range(ROW_SPLIT)]


def _ffn_half_step(h, g, w_in_ref, w_out_ref, act_ref):
    xn = _rms(h, g).astype(BF16)
    for c in range(D_FF // FFN_COLS):
        lo = c * FFN_COLS
        gate = jnp.dot(xn, w_in_ref[:, lo:lo + FFN_COLS], preferred_element_type=F32)
        up = jnp.dot(xn, w_in_ref[:, D_FF + lo:D_FF + lo + FFN_COLS],
                     preferred_element_type=F32)
        act_ref[:, lo:lo + FFN_COLS] = (gate * _sigmoid(gate) * up).astype(BF16)
    ys = [jnp.dot(act_ref[r, :], w_out_ref[...], preferred_element_type=F32) for r in ROW_GROUPS]
    return [h[r] + 0.5 * y for r, y in zip(ROW_GROUPS, ys)]


def _resident(block_shape, index):
    return pl.BlockSpec(block_shape, lambda *_: index, pipeline_mode=pl.Buffered(1))


def _params(n_axes):
    return pltpu.CompilerParams(
        dimension_semantics=("arbitrary",) * n_axes, vmem_limit_bytes=VMEM_LIMIT)


def _ffn_specs(layer, half):
    return [
        _resident((None, None, 1, D_MODEL), (layer, half, 0, 0)),
        _resident((D_MODEL, 2 * D_FF), (0, 0)),
        _resident((D_FF, D_MODEL), (0, 0)),
    ]


def _pre_mla_kernel(h_ref, fg_ref, w_in_ref, w_out_ref, pos_ref, g_ref, wd_ref, qn_ref, wuq_ref,
                    kvn_ref, wukv_ref, invf_ref, ho_ref, q_ref, k_ref, v_ref, act_ref):
    scale = (MLA_NOPE + MLA_ROPE) ** -0.5 * LOG2E
    hs = _ffn_half_step(h_ref[...], fg_ref[...], w_in_ref, w_out_ref, act_ref)
    for r, h in zip(ROW_GROUPS, hs):
        ho_ref[r, :] = h

    ang = pos_ref[...].astype(F32) * invf_ref[...]
    cosv = jnp.cos(ang)
    sinv = jnp.sin(ang)
    half = MLA_ROPE // 2
    lane = lax.broadcasted_iota(jnp.int32, (TM // ROPE_PACK, LANES), 1)
    cos_rows, sin_rows = [], []
    for a in range(ROPE_PACK):
        lo_shift = (-a * half) % LANES
        hi_shift = (half - a * half) % LANES

        def table(v):
            return jnp.where(lane < half, _lane_roll(v, lo_shift),
                             jnp.where(lane < MLA_ROPE, _lane_roll(v, hi_shift), 0.0))

        cos_rows.append(table(cosv))
        sin_rows.append(table(sinv))
    cos_t = jnp.concatenate(cos_rows, axis=0)
    sin_t = jnp.concatenate(sin_rows, axis=0)

    rows = ROW_GROUPS
    cs = [jnp.dot(_rms(h, g_ref[...]).astype(BF16), wd_ref[...],
                  preferred_element_type=F32) for h in hs]
    lins = []
    for c in cs:
        cq = _rms(c[:, :MLA_Q_LORA], qn_ref[...]).astype(BF16)
        ckv = _rms(c[:, MLA_Q_LORA:MLA_Q_LORA + MLA_KV_LORA], kvn_ref[...]).astype(BF16)
        lins.append((jnp.dot(cq, wuq_ref[...], preferred_element_type=F32),
                     jnp.dot(ckv, wukv_ref[...], preferred_element_type=F32)))
    for r, c, (q_lin, kv) in zip(rows, cs, lins):
        def rope(x):
            return x * cos_t[r] + pltpu.roll(x, MLA_ROPE, 1) * sin_t[r]

        k_rope = rope(c[:, MLA_Q_LORA + MLA_KV_LORA:]).astype(BF16)
        for hh in range(HEADS):
            q_nope = q_lin[:, hh * LANES:(hh + 1) * LANES]
            q_rot = q_lin[:, (HEADS + hh) * LANES:(HEADS + hh + 1) * LANES]
            q_ref[r, hh * QK_DIM:hh * QK_DIM + LANES] = (q_nope * scale).astype(BF16)
            q_ref[r, hh * QK_DIM + LANES:(hh + 1) * QK_DIM] = (rope(q_rot) * scale).astype(BF16)
            k_ref[r, hh * QK_DIM:hh * QK_DIM + LANES] = kv[:, hh * 256:hh * 256 + 128].astype(BF16)
            k_ref[r, hh * QK_DIM + LANES:(hh + 1) * QK_DIM] = k_rope
            v_ref[r, hh * V_DIM:(hh + 1) * V_DIM] = (
                kv[:, hh * 256 + 128:(hh + 1) * 256].astype(BF16))


def _pre_mla_call(h, ffn_norm, w_in, w_out, pos, mix_norm, wd, qn, wuq, kvn, wukv, invf, layer, j):
    t = h.shape[0]
    tok = lambda w: pl.BlockSpec((TM, w), lambda i: (i, 0))
    return pl.pallas_call(
        _pre_mla_kernel,
        grid=(t // TM,),
        in_specs=[
            tok(D_MODEL),
            *_ffn_specs(layer, 0),
            pl.BlockSpec((None, TM // ROPE_PACK, LANES), lambda i: (i, 0, 0)),
            _resident((None, 1, D_MODEL), (layer, 0, 0)),
            _resident((None, D_MODEL, 512), (j, 0, 0)),
            _resident((None, 1, MLA_Q_LORA), (j, 0, 0)),
            _resident((None, MLA_Q_LORA, 2 * HEADS * LANES), (j, 0, 0)),
            _resident((None, 1, MLA_KV_LORA), (j, 0, 0)),
            _resident((None, MLA_KV_LORA, HEADS * (MLA_NOPE + MLA_V)), (j, 0, 0)),
            _resident((1, LANES), (0, 0)),
        ],
        out_specs=[tok(D_MODEL), tok(HEADS * QK_DIM), tok(HEADS * QK_DIM), tok(HEADS * V_DIM)],
        out_shape=[
            jax.ShapeDtypeStruct((t, D_MODEL), F32),
            jax.ShapeDtypeStruct((t, HEADS * QK_DIM), BF16),
            jax.ShapeDtypeStruct((t, HEADS * QK_DIM), BF16),
            jax.ShapeDtypeStruct((t, HEADS * V_DIM), BF16),
        ],
        scratch_shapes=[pltpu.VMEM((TM, D_FF), BF16)],
        compiler_params=_params(1),
        name="pre_mla",
    )(h, ffn_norm, w_in, w_out, pos, mix_norm, wd, qn, wuq, kvn, wukv, invf)


def _split3(x):
    hi = x.astype(BF16).astype(F32)
    r = x - hi
    mid = r.astype(BF16).astype(F32)
    lo = r - mid
    return hi, mid, lo


def _pre_fox_kernel(h_ref, fg_ref, w_in_ref, w_out_ref, g_ref, w_ref, bf_ref,
                    ho_ref, q_ref, k_ref, v_ref, act_ref, carry_ref, tri_ref):
    hd = HEADS * FOX_HEAD_DIM
    scale = FOX_HEAD_DIM ** -0.5 * LOG2E

    n_rows = TM // ROW_SPLIT

    @pl.when((pl.program_id(0) == 0) & (pl.program_id(1) == 0))
    def _():
        r = lax.broadcasted_iota(jnp.int32, (n_rows, n_rows), 0)
        cc = lax.broadcasted_iota(jnp.int32, (n_rows, n_rows), 1)
        tri_ref[...] = (cc <= r).astype(BF16)

    @pl.when(pl.program_id(1) == 0)
    def _():
        carry_ref[...] = jnp.zeros_like(carry_ref)

    hs = _ffn_half_step(h_ref[...], fg_ref[...], w_in_ref, w_out_ref, act_ref)
    lane = lax.broadcasted_iota(jnp.int32, (n_rows, LANES), 1)
    carry = carry_ref[...]
    for r, h in zip(ROW_GROUPS, hs):
        ho_ref[r, :] = h

        u = _rms(h, g_ref[...]).astype(BF16)
        f_logit = jnp.dot(u, w_ref[:, 3 * hd:], preferred_element_type=F32)
        q = jnp.dot(u, w_ref[:, :hd], preferred_element_type=F32)
        z = f_logit + bf_ref[...]
        log_f = -(jnp.maximum(-z, 0.0) + jnp.log1p(jnp.exp(-jnp.abs(z))))

        hi, mid, lo = _split3(log_f)
        packed = jnp.where(lane < HEADS, hi,
                           jnp.where(lane < 2 * HEADS, _lane_roll(mid, HEADS),
                                     jnp.where(lane < 3 * HEADS, _lane_roll(lo, 2 * HEADS), 0.0)))
        k = jnp.dot(u, w_ref[:, hd:2 * hd], preferred_element_type=F32)
        psum = jnp.dot(tri_ref[...], packed.astype(BF16), preferred_element_type=F32)
        cum = ((_lane_roll(psum, -2 * HEADS) + _lane_roll(psum, -HEADS)) + psum) + carry
        carry = cum[n_rows - 1:n_rows, :]

        cum2 = cum * LOG2E
        parts = []
        for hh in range(HEADS):
            cb = jnp.broadcast_to(cum2[:, hh:hh + 1], (n_rows, LANES))
            c_hi, c_mid, c_lo = _split3(cb)
            parts.append(jnp.where((lane == 0) | (lane == 3), c_hi,
                                   jnp.where((lane == 1) | (lane == 4), c_mid, c_lo)))
            q_ex = jnp.where(lane < 3, parts[hh], jnp.where(lane < 6, 1.0, 0.0))
            q_ref[r, hh * QK_DIM:hh * QK_DIM + LANES] = (
                q[:, hh * LANES:(hh + 1) * LANES] * scale).astype(BF16)
            q_ref[r, hh * QK_DIM + LANES:(hh + 1) * QK_DIM] = q_ex.astype(BF16)
        v = jnp.dot(u, w_ref[:, 2 * hd:3 * hd], preferred_element_type=F32)
        for hh in range(HEADS):
            k_ex = jnp.where(lane < 3, 1.0, jnp.where(lane < 6, -parts[hh], 0.0))
            k_ref[r, hh * QK_DIM:hh * QK_DIM + LANES] = (
                k[:, hh * LANES:(hh + 1) * LANES].astype(BF16))
            k_ref[r, hh * QK_DIM + LANES:(hh + 1) * QK_DIM] = k_ex.astype(BF16)
        v_ref[r, :] = v.astype(BF16)
    carry_ref[...] = carry


def _pre_fox_call(h, ffn_norm, w_in, w_out, mix_norm, fox_w, b_f, layer, j, batch, seq):
    t = h.shape[0]
    nt = seq // TM
    tok = lambda w: pl.BlockSpec((TM, w), lambda b, i: (b * nt + i, 0))
    return pl.pallas_call(
        _pre_fox_kernel,
        grid=(batch, nt),
        in_specs=[
            tok(D_MODEL),
            *_ffn_specs(layer, 0),
            _resident((None, 1, D_MODEL), (layer, 0, 0)),
            _resident((None, D_MODEL, 3 * HEADS * FOX_HEAD_DIM + LANES), (j, 0, 0)),
            _resident((None, 1, LANES), (j, 0, 0)),
        ],
        out_specs=[tok(D_MODEL), tok(HEADS * QK_DIM), tok(HEADS * QK_DIM), tok(HEADS * V_DIM)],
        out_shape=[
            jax.ShapeDtypeStruct((t, D_MODEL), F32),
            jax.ShapeDtypeStruct((t, HEADS * QK_DIM), BF16),
            jax.ShapeDtypeStruct((t, HEADS * QK_DIM), BF16),
            jax.ShapeDtypeStruct((t, HEADS * V_DIM), BF16),
        ],
        scratch_shapes=[pltpu.VMEM((TM, D_FF), BF16), pltpu.VMEM((1, LANES), F32),
                        pltpu.VMEM((TM // ROW_SPLIT, TM // ROW_SPLIT), BF16)],
        compiler_params=_params(2),
        name="pre_fox",
    )(h, ffn_norm, w_in, w_out, mix_norm, fox_w, b_f)


def _attn_kernel(q_ref, k_ref, v_ref, *rest, chunk_causal, seq, n_cast):
    o_ref, vt_ref = rest[n_cast], rest[-1]
    for src, dst in zip(rest[:n_cast], rest[n_cast + 1:-1]):
        dst[...] = src[...].astype(BF16)

    key = lax.broadcasted_iota(jnp.int32, (TQ, TQ), 0)
    qry = lax.broadcasted_iota(jnp.int32, (TQ, TQ), 1)
    if chunk_causal:
        shift = CHUNK.bit_length() - 1
        allowed = (key >> shift) <= (qry >> shift)
    else:
        allowed = key <= qry
    nt = (((1,), (1,)), ((), ()))

    for hh in range(HEADS_PER_STEP):
        vt_ref[hh, :, :] = v_ref[:, hh * V_DIM:(hh + 1) * V_DIM].astype(F32).T.astype(BF16)

    def scores(hh, i):
        q0 = i * TQ
        qk = slice(hh * QK_DIM, (hh + 1) * QK_DIM)
        q = q_ref[q0:q0 + TQ, qk]
        s_dg = lax.dot_general(k_ref[q0:q0 + TQ, qk], q, nt, preferred_element_type=F32)
        s_dg = jnp.where(allowed, s_dg, NEG_INF)
        s_lo = None
        if i > 0:
            s_lo = lax.dot_general(k_ref[0:q0, qk], q, nt, preferred_element_type=F32)
        return s_dg, s_lo

    def finish(hh, i, s_dg, s_lo):
        q0 = i * TQ
        m = jnp.max(s_dg, axis=0, keepdims=True)
        if i > 0:
            m = jnp.maximum(m, jnp.max(s_lo, axis=0, keepdims=True))
        p_dg = jnp.exp2(s_dg - m)
        l = jnp.sum(p_dg, axis=0, keepdims=True)
        o_t = jnp.dot(vt_ref[hh, :, q0:q0 + TQ], p_dg.astype(BF16), preferred_element_type=F32)
        if i > 0:
            p_lo = jnp.exp2(s_lo - m)
            l = l + jnp.sum(p_lo, axis=0, keepdims=True)
            o_t = o_t + jnp.dot(vt_ref[hh, :, 0:q0], p_lo.astype(BF16),
                                preferred_element_type=F32)
        out = o_t / l
        o_ref[q0:q0 + TQ, hh * V_DIM:(hh + 1) * V_DIM] = out.T.astype(o_ref.dtype)

    items = [(hh, i) for hh in range(HEADS_PER_STEP) for i in range(seq // TQ)]
    pending = [scores(*it) for it in items[:SCORE_LOOKAHEAD]]
    for n, item in enumerate(items):
        if n + SCORE_LOOKAHEAD < len(items):
            pending.append(scores(*items[n + SCORE_LOOKAHEAD]))
        finish(*item, *pending.pop(0))


def _attn_call(q, k, v, chunk_causal, casts):
    batch, seq, _ = q.shape
    hsteps = HEADS // HEADS_PER_STEP
    steps = batch * hsteps
    cast_in, cast_specs_in, cast_specs_out, cast_shapes = [], [], [], []
    for w, lead in casts:
        n_rows, cols = w.shape[-2:]
        rows = n_rows // steps
        assert rows * steps == n_rows and rows % BF16_SUBLANES == 0
        cast_in.append(w)
        cast_specs_in.append(pl.BlockSpec(
            (None,) * len(lead) + (rows, cols),
            lambda b, h, lead=lead: (*lead, b * hsteps + h, 0)))
        cast_specs_out.append(pl.BlockSpec((rows, cols), lambda b, h: (b * hsteps + h, 0)))
        cast_shapes.append(jax.ShapeDtypeStruct((n_rows, cols), BF16))
    n_cast = len(cast_in)
    head_block = lambda w: pl.BlockSpec((None, seq, HEADS_PER_STEP * w), lambda b, h: (b, 0, h))
    outs = pl.pallas_call(
        functools.partial(_attn_kernel, chunk_causal=chunk_causal, seq=seq, n_cast=n_cast),
        grid=(batch, hsteps),
        in_specs=[head_block(QK_DIM), head_block(QK_DIM), head_block(V_DIM), *cast_specs_in],
        out_specs=[head_block(V_DIM), *cast_specs_out],
        out_shape=[jax.ShapeDtypeStruct((batch, seq, HEADS * V_DIM), BF16), *cast_shapes],
        scratch_shapes=[pltpu.VMEM((HEADS_PER_STEP, V_DIM, seq), BF16)],
        compiler_params=_params(2),
        name="attn",
    )(q, k, v, *cast_in)
    return outs[0], list(outs[1:])


def _post_kernel(h_ref, a_ref, p_ref, wo_ref, g_ref, w_in_ref, w_out_ref, pg_ref, wg_ref,
                 wp_ref, fn_ref, o_ref, act_ref, *, final):
    h = h_ref[...] + jnp.dot(a_ref[...], wo_ref[...], preferred_element_type=F32)
    hs = _ffn_half_step(h, g_ref[...], w_in_ref, w_out_ref, act_ref)
    for r, h in zip(ROW_GROUPS, hs):
        gate = _sigmoid(jnp.dot(_rms(h, pg_ref[...]).astype(BF16), wg_ref[...],
                                preferred_element_type=F32))
        emb = jnp.dot(p_ref[r, :].astype(BF16), wp_ref[...], preferred_element_type=F32)
        h = h + gate * emb
        if final:
            h = _rms(h, fn_ref[...])
        o_ref[r, :] = h


def _post_call(h, attn, p, w_o, ffn_norm, w_in, w_out, ple_norm, w_gate, w_proj, final_norm,
               layer, final):
    t = h.shape[0]
    return pl.pallas_call(
        functools.partial(_post_kernel, final=final),
        grid=(t // TM,),
        in_specs=[
            pl.BlockSpec((TM, D_MODEL), lambda i: (i, 0)),
            pl.BlockSpec((TM, HEADS * V_DIM), lambda i: (i, 0)),
            pl.BlockSpec((None, TM, PLE_DIM), lambda i: (layer, i, 0)),
            _resident((HEADS * V_DIM, D_MODEL), (0, 0)),
            *_ffn_specs(layer, 1),
            _resident((None, 1, D_MODEL), (layer, 0, 0)),
            _resident((D_MODEL, D_MODEL), (0, 0)),
            _resident((PLE_DIM, D_MODEL), (0, 0)),
            _resident((1, D_MODEL), (0, 0)),
        ],
        out_specs=pl.BlockSpec((TM, D_MODEL), lambda i: (i, 0)),
        out_shape=jax.ShapeDtypeStruct((t, D_MODEL), F32),
        scratch_shapes=[pltpu.VMEM((TM, D_FF), BF16)],
        compiler_params=_params(1),
        name="post",
    )(h, attn, p, w_o, ffn_norm, w_in, w_out, ple_norm, w_gate, w_proj, final_norm)


def kernel(x, p, positions, ffn_norm, ffn_w_in, ffn_w_out, mix_norm, mla_w_down, mla_q_norm,
           mla_w_uq, mla_kv_norm, mla_w_ukv, mla_w_o, fox_w_in, fox_b_f, fox_w_o, ple_norm,
           ple_w_gate, ple_w_proj, final_norm):
    batch, seq, _ = x.shape
    t = batch * seq
    n_mla = mla_w_down.shape[0]
    n_fox = fox_w_in.shape[0]

    pre_w = (ffn_w_in[0, 0].astype(BF16), ffn_w_out[0, 0].astype(BF16))
    ffn_g = ffn_norm.reshape(DEPTH, 2, 1, D_MODEL)
    mix_g = mix_norm.reshape(DEPTH, 1, D_MODEL)
    ple_g = ple_norm.reshape(DEPTH, 1, D_MODEL)
    fin_g = final_norm.reshape(1, D_MODEL)

    def with_partner(w_rope):
        x1, x2 = w_rope[..., :MLA_ROPE // 2], w_rope[..., MLA_ROPE // 2:]
        return jnp.concatenate([w_rope, -x2, x1], axis=-1)

    kv_end = MLA_Q_LORA + MLA_KV_LORA
    wd = jnp.concatenate(
        [mla_w_down[..., :kv_end], with_partner(mla_w_down[..., kv_end:])], axis=-1).astype(BF16)
    wq = mla_w_uq.reshape(n_mla, MLA_Q_LORA, HEADS, MLA_NOPE + MLA_ROPE)
    wq_nope = wq[..., :MLA_NOPE].reshape(n_mla, MLA_Q_LORA, HEADS * LANES)
    wq_rope = with_partner(wq[..., MLA_NOPE:])
    wuq = jnp.concatenate(
        [wq_nope, wq_rope.reshape(n_mla, MLA_Q_LORA, HEADS * LANES)], axis=-1).astype(BF16)
    wukv = mla_w_ukv.astype(BF16)
    q_g = mla_q_norm.reshape(n_mla, 1, MLA_Q_LORA)
    kv_g = mla_kv_norm.reshape(n_mla, 1, MLA_KV_LORA)
    inv_freq = ROPE_THETA ** (-jnp.arange(0, MLA_ROPE, 2, dtype=F32) / MLA_ROPE)
    invf = jnp.tile(inv_freq, ROPE_PACK)[None, :]

    fox_w = jnp.pad(fox_w_in, ((0, 0), (0, 0), (0, LANES - HEADS))).astype(BF16)
    fox_b = jnp.pad(fox_b_f, ((0, 0), (0, LANES - HEADS))).reshape(n_fox, 1, LANES)

    pos = positions.reshape(t // TM, ROPE_PACK, TM // ROPE_PACK).transpose(0, 2, 1)
    pos = jnp.repeat(pos, MLA_ROPE // 2, axis=2)
    p_tok = p.reshape(DEPTH, t, PLE_DIM)

    h = x.reshape(t, D_MODEL)
    for i in range(DEPTH):
        j = i // 2
        if i % 2 == 0:
            h, q, k, v = _pre_mla_call(h, ffn_g, *pre_w, pos, mix_g, wd, q_g, wuq, kv_g,
                                       wukv, invf, i, j)
            w_o = mla_w_o
        else:
            h, q, k, v = _pre_fox_call(h, ffn_g, *pre_w, mix_g, fox_w, fox_b, i, j,
                                       batch, seq)
            w_o = fox_w_o
        casts = [(w_o, (j,)), (ffn_w_in, (i, 1)), (ffn_w_out, (i, 1)), (ple_w_gate, (i,)),
                 (ple_w_proj, (i,))]
        if i + 1 < DEPTH:
            casts += [(ffn_w_in, (i + 1, 0)), (ffn_w_out, (i + 1, 0))]
        attn, cast = _attn_call(q.reshape(batch, seq, -1), k.reshape(batch, seq, -1),
                                v.reshape(batch, seq, -1), i % 2 == 0, casts)
        wo_b, w_in_b, w_out_b, w_gate_b, w_proj_b = cast[:5]
        pre_w = tuple(cast[5:])
        h = _post_call(h, attn.reshape(t, -1), p_tok, wo_b, ffn_g, w_in_b, w_out_b, ple_g,
                       w_gate_b, w_proj_b, fin_g, i, final=(i == DEPTH - 1))
    return h.reshape(batch, seq, D_MODEL)
```
